```python
import math
import jax
import jax.numpy as jnp
from jax import lax
import numpy as np

D_MODEL = 1024
BATCH = 32
SEQ = 2048
DEPTH = 1
DEC_BATCH = 32
DEC_SEQ = 64
PAST_LEN = 2048

CHUNK = 64
MIX_WIDTH = D_MODEL
R_WIDTH = MIX_WIDTH // 2
R_HEAD = 64
R_HEADS = R_WIDTH // R_HEAD
DECAY_LORA = 64
AAA_LORA = 64
GATE_LORA = 128
R_PROJ = 3 * R_WIDTH + DECAY_LORA + AAA_LORA + GATE_LORA
R_GN_EPS = R_HEAD * 1e-5
M_WIDTH = MIX_WIDTH - R_WIDTH
M_HEADDIM = 64
M_HEADS = M_WIDTH // M_HEADDIM
M_GROUPS = 2
M_STATE = 64
CONV_W = 4
CONV_DIM = M_WIDTH + 2 * M_GROUPS * M_STATE
M_PROJ = M_WIDTH + CONV_DIM + M_HEADS
IN_PROJ = R_PROJ + M_PROJ
N_KEYS = 128
N_EXPERTS = N_KEYS * N_KEYS
P_HEADS = 8
P_QDIM = 256
P_TOPK = 16
PEER_BLOCK = 256
NORM_EPS = 1e-6

kernel_name = 'hymba_rwkv7_mamba2_peer_stream_step'


def rmsnorm(x, w):
    xf = x.astype(jnp.float32)
    y = xf * lax.rsqrt(jnp.mean(xf * xf, axis=-1, keepdims=True) + NORM_EPS)
    return (y * w.astype(jnp.float32)).astype(x.dtype)


def rwkv7_mixer(p, shift_state, wkv_state, mu, w0, w2, a0, a2, g2, k_k, k_a, r_k, ln_w, ln_b):
    bsz, t, _ = p.shape
    f32 = jnp.float32
    p_prev = jnp.concatenate([shift_state.astype(p.dtype)[:, None, :], p[:, :-1]], axis=1)
    pm = p + (p_prev - p) * mu
    r, k, v, wd, ad, gd = jnp.split(pm, [R_WIDTH, 2 * R_WIDTH, 3 * R_WIDTH, 3 * R_WIDTH + DECAY_LORA,
                                         3 * R_WIDTH + DECAY_LORA + AAA_LORA], axis=-1)
    w_log = -jax.nn.softplus(-(w0 + jnp.tanh(wd) @ w2).astype(f32)) - 0.5
    decay = jnp.exp(-jnp.exp(w_log))
    a = jax.nn.sigmoid((a0 + ad @ a2).astype(f32))
    g = (jax.nn.sigmoid(gd) @ g2).astype(f32)
    heads = lambda z: z.astype(f32).reshape(bsz, t, R_HEADS, R_HEAD)
    kk = heads(k * k_k)
    kk = kk / jnp.maximum(jnp.sqrt(jnp.sum(kk * kk, axis=-1, keepdims=True)), 1e-12)
    k_mod = heads(k.astype(f32) * (1.0 + (a - 1.0) * k_a.astype(f32)))
    rh, vh, wh, ah = heads(r), heads(v), heads(decay), heads(a)
    a_vec = -kk
    b_vec = kk * ah

    def step(S, inp):
        r_t, w_t, k_t, v_t, a_t, b_t = inp
        sa = jnp.einsum('bhvk,bhk->bhv', S, a_t)
        S = S * w_t[:, :, None, :] + sa[..., None] * b_t[:, :, None, :] + v_t[..., None] * k_t[:, :, None, :]
        return S, jnp.einsum('bhvk,bhk->bhv', S, r_t)

    seq_in = tuple(jnp.moveaxis(z, 1, 0) for z in (rh, wh, k_mod, vh, a_vec, b_vec))
    s_final, o = lax.scan(step, wkv_state.astype(f32), seq_in)
    o = jnp.moveaxis(o, 0, 1)
    mean = jnp.mean(o, axis=-1, keepdims=True)
    var = jnp.mean(jnp.square(o - mean), axis=-1, keepdims=True)
    o = ((o - mean) * lax.rsqrt(var + R_GN_EPS)).reshape(bsz, t, R_WIDTH)
    o = o * ln_w.astype(f32) + ln_b.astype(f32)
    bonus = jnp.sum(rh * k_mod * r_k.astype(f32), axis=-1, keepdims=True) * vh
    o = (o + bonus.reshape(bsz, t, R_WIDTH)) * g
    return o.astype(p.dtype), p[:, -1], s_final.astype(p.dtype)


def ssd_chunked(x, dt, A, Bm, Cm, h0):
    bsz, t, nh, hp = x.shape
    L = min(CHUNK, t)
    nc = t // L
    R = nh // M_GROUPS
    xdt = (x * dt[..., None]).reshape(bsz, nc, L, M_GROUPS, R, hp)
    cs = jnp.cumsum((dt * A).reshape(bsz, nc, L, M_GROUPS, R), axis=2)
    Bc = Bm.reshape(bsz, nc, L, M_GROUPS, M_STATE)
    Cc = Cm.reshape(bsz, nc, L, M_GROUPS, M_STATE)
    causal = jnp.tril(jnp.ones((L, L), dtype=bool))[None, None, :, :, None, None]
    seg = cs[:, :, :, None] - cs[:, :, None, :]
    decay_ls = jnp.exp(jnp.where(causal, seg, -jnp.inf))
    cb = jnp.einsum('bclgn,bcsgn->bclsg', Cc, Bc)
    y_diag = jnp.einsum('bclsg,bclsgr,bcsgrp->bclgrp', cb, decay_ls, xdt)
    to_end = jnp.exp(cs[:, :, -1:] - cs)
    chunk_states = jnp.einsum('bclgn,bclgr,bclgrp->bcgrpn', Bc, to_end, xdt)
    chunk_decay = jnp.exp(cs[:, :, -1])

    def step(h, inp):
        dec, st = inp
        return h * dec[..., None, None] + st, h

    h_last, h_in = lax.scan(step, h0.reshape(bsz, M_GROUPS, R, hp, M_STATE),
                            (jnp.moveaxis(chunk_decay, 1, 0), jnp.moveaxis(chunk_states, 1, 0)))
    h_in = jnp.moveaxis(h_in, 0, 1)
    y_off = jnp.einsum('bclgn,bcgrpn,bclgr->bclgrp', Cc, h_in, jnp.exp(cs))
    return (y_diag + y_off).reshape(bsz, t, nh, hp), h_last.reshape(bsz, nh, hp, M_STATE)


def mamba2_mixer(pm, conv_state, ssm_state, conv_w, conv_b, dt_bias, A_log, D, norm_w):
    bsz, t, _ = pm.shape
    f32 = jnp.float32
    z, xbc, dt = jnp.split(pm, [M_WIDTH, M_WIDTH + CONV_DIM], axis=-1)
    xbc_full = jnp.concatenate([conv_state.astype(xbc.dtype), xbc], axis=1)
    conv = lax.conv_general_dilated(xbc_full, conv_w.astype(xbc.dtype)[:, None, :], window_strides=(1,),
                                    padding='VALID', dimension_numbers=('NWC', 'WIO', 'NWC'),
                                    feature_group_count=CONV_DIM)
    xbc_act = jax.nn.silu((conv + conv_b).astype(f32))
    xs, Bm, Cm = jnp.split(xbc_act, [M_WIDTH, M_WIDTH + M_GROUPS * M_STATE], axis=-1)
    xs = xs.reshape(bsz, t, M_HEADS, M_HEADDIM)
    dt = jax.nn.softplus(dt.astype(f32) + dt_bias.astype(f32))
    A = -jnp.exp(A_log.astype(f32))
    y, h_last = ssd_chunked(xs, dt, A, Bm.reshape(bsz, t, M_GROUPS, M_STATE),
                            Cm.reshape(bsz, t, M_GROUPS, M_STATE), ssm_state.astype(f32))
    y = (y + D.astype(f32)[:, None] * xs).reshape(bsz, t, M_WIDTH) * jax.nn.silu(z.astype(f32))
    yg = y.reshape(bsz, t, M_GROUPS, M_WIDTH // M_GROUPS)
    yg = yg * lax.rsqrt(jnp.mean(yg * yg, axis=-1, keepdims=True) + NORM_EPS)
    y = yg.reshape(bsz, t, M_WIDTH) * norm_w.astype(f32)
    return y.astype(pm.dtype), xbc_full[:, -(CONV_W - 1):], h_last.astype(pm.dtype)


def peer_ffn(xn, w_q, sub_keys, u_tab, v_tab):
    bsz, t, _ = xn.shape
    q = (xn @ w_q).reshape(bsz, t, P_HEADS, 2, P_QDIM // 2)
    s = jnp.einsum('bthcd,ckd->bthck', q, sub_keys).astype(jnp.float32)
    sv, si = lax.top_k(s, P_TOPK)
    cand = (sv[..., 0, :, None] + sv[..., 1, None, :]).reshape(bsz, t, P_HEADS, P_TOPK * P_TOPK)
    cand_id = (si[..., 0, :, None] * N_KEYS + si[..., 1, None, :]).reshape(bsz, t, P_HEADS, P_TOPK * P_TOPK)
    top_s, top_pos = lax.top_k(cand, P_TOPK)
    eid = jnp.take_along_axis(cand_id, top_pos, axis=-1)
    gate = jax.nn.softmax(top_s, axis=-1).astype(xn.dtype)
    n = bsz * t
    nb = -(-n // PEER_BLOCK)
    pad = nb * PEER_BLOCK - n
    xt = jnp.pad(xn.reshape(n, D_MODEL), ((0, pad), (0, 0))).reshape(nb, PEER_BLOCK, D_MODEL)
    ei = jnp.pad(eid.reshape(n, P_HEADS * P_TOPK), ((0, pad), (0, 0))).reshape(nb, PEER_BLOCK, P_HEADS * P_TOPK)
    gt = jnp.pad(gate.reshape(n, P_HEADS * P_TOPK), ((0, pad), (0, 0))).reshape(nb, PEER_BLOCK, P_HEADS * P_TOPK)

    def block(args):
        xb, eb, gb = args
        hid = jnp.einsum('td,ted->te', xb, u_tab[eb])
        return jnp.einsum('te,ted->td', jax.nn.gelu(hid, approximate=False) * gb, v_tab[eb])

    out = lax.map(block, (xt, ei, gt))
    return out.reshape(nb * PEER_BLOCK, D_MODEL)[:n].reshape(bsz, t, D_MODEL)


def trunk_layer(x, shift_st, wkv_st, conv_st, ssm_st, norm1_w, w_in, rwkv_mu, rwkv_w0, rwkv_w2, rwkv_a0,
                rwkv_a2, rwkv_g2, rwkv_k_k, rwkv_k_a, rwkv_r_k, rwkv_ln_w, rwkv_ln_b, mamba_conv_w,
                mamba_conv_b, mamba_dt_bias, mamba_A_log, mamba_D, mamba_norm_w, w_out, norm2_w, peer_w_q,
                peer_sub_keys, peer_u, peer_v):
    proj = rmsnorm(x, norm1_w) @ w_in
    o_r, new_shift, new_wkv = rwkv7_mixer(proj[..., :R_PROJ], shift_st, wkv_st, rwkv_mu, rwkv_w0, rwkv_w2,
                                          rwkv_a0, rwkv_a2, rwkv_g2, rwkv_k_k, rwkv_k_a, rwkv_r_k,
                                          rwkv_ln_w, rwkv_ln_b)
    o_m, new_conv, new_ssm = mamba2_mixer(proj[..., R_PROJ:], conv_st, ssm_st, mamba_conv_w, mamba_conv_b,
                                          mamba_dt_bias, mamba_A_log, mamba_D, mamba_norm_w)
    h = x + jnp.concatenate([o_r, o_m], axis=-1) @ w_out
    h = h + peer_ffn(rmsnorm(h, norm2_w), peer_w_q, peer_sub_keys, peer_u, peer_v)
    return h, new_shift, new_wkv, new_conv, new_ssm


def setup_inputs(seed: int = 0) -> dict:
    key = jax.random.key(seed)
    ks = jax.random.split(key, 40)
    nrm = lambda k, shape, scale: jax.random.normal(k, shape, jnp.float32) * scale
    dt0 = jnp.exp(jax.random.uniform(ks[20], (DEPTH, M_HEADS), jnp.float32) * (math.log(0.1) - math.log(0.001))
                  + math.log(0.001))
    return {
        'x_prompt': nrm(ks[0], (BATCH, SEQ, D_MODEL), 1.0),
        'x_sample': nrm(ks[1], (DEC_BATCH, DEC_SEQ, D_MODEL), 1.0),
        'state_rwkv_shift': nrm(ks[2], (DEPTH, DEC_BATCH, R_PROJ), 1.0),
        'state_rwkv_wkv': nrm(ks[3], (DEPTH, DEC_BATCH, R_HEADS, R_HEAD, R_HEAD), 0.5),
        'state_mamba_conv': nrm(ks[4], (DEPTH, DEC_BATCH, CONV_W - 1, CONV_DIM), 1.0),
        'state_mamba_ssm': nrm(ks[5], (DEPTH, DEC_BATCH, M_HEADS, M_HEADDIM, M_STATE), 0.5),
        'norm1_w': 1.0 + nrm(ks[6], (DEPTH, D_MODEL), 0.01),
        'w_in': nrm(ks[7], (DEPTH, D_MODEL, IN_PROJ), D_MODEL ** -0.5),
        'rwkv_mu': jax.random.uniform(ks[8], (DEPTH, R_PROJ), jnp.float32, 0.2, 0.8),
        'rwkv_w0': nrm(ks[9], (DEPTH, R_WIDTH), 0.5),
        'rwkv_w2': nrm(ks[10], (DEPTH, DECAY_LORA, R_WIDTH), 0.5 * DECAY_LORA ** -0.5),
        'rwkv_a0': nrm(ks[11], (DEPTH, R_WIDTH), 0.5),
        'rwkv_a2': nrm(ks[12], (DEPTH, AAA_LORA, R_WIDTH), 0.5 * AAA_LORA ** -0.5),
        'rwkv_g2': nrm(ks[13], (DEPTH, GATE_LORA, R_WIDTH), GATE_LORA ** -0.5),
        'rwkv_k_k': 0.85 + nrm(ks[14], (DEPTH, R_WIDTH), 0.1),
        'rwkv_k_a': 1.0 + nrm(ks[15], (DEPTH, R_WIDTH), 0.1),
        'rwkv_r_k': nrm(ks[16], (DEPTH, R_HEADS, R_HEAD), 0.1),
        'rwkv_ln_w': 1.0 + nrm(ks[17], (DEPTH, R_WIDTH), 0.01),
        'rwkv_ln_b': nrm(ks[18], (DEPTH, R_WIDTH), 0.01),
        'mamba_conv_w': nrm(ks[19], (DEPTH, CONV_W, CONV_DIM), CONV_W ** -0.5),
        'mamba_conv_b': nrm(ks[21], (DEPTH, CONV_DIM), 0.01),
        'mamba_dt_bias': dt0 + jnp.log(-jnp.expm1(-dt0)),
        'mamba_A_log': jnp.log(jax.random.uniform(ks[22], (DEPTH, M_HEADS), jnp.float32, 1.0, 16.0)),
        'mamba_D': 1.0 + nrm(ks[23], (DEPTH, M_HEADS), 0.1),
        'mamba_norm_w': 1.0 + nrm(ks[24], (DEPTH, M_WIDTH), 0.01),
        'w_out': nrm(ks[25], (DEPTH, MIX_WIDTH, D_MODEL), MIX_WIDTH ** -0.5),
        'norm2_w': 1.0 + nrm(ks[26], (DEPTH, D_MODEL), 0.01),
        'peer_w_q': nrm(ks[27], (DEPTH, D_MODEL, P_HEADS * P_QDIM), D_MODEL ** -0.5),
        'peer_sub_keys': nrm(ks[28], (DEPTH, 2, N_KEYS, P_QDIM // 2), (P_QDIM // 2) ** -0.5),
        'peer_u': nrm(ks[29], (DEPTH, N_EXPERTS, D_MODEL), D_MODEL ** -0.5),
        'peer_v': nrm(ks[30], (DEPTH, N_EXPERTS, D_MODEL), 0.3),
        'final_norm_w': 1.0 + nrm(ks[31], (D_MODEL,), 0.01),
    }


def reference(x_prompt, x_sample, state_rwkv_shift, state_rwkv_wkv, state_mamba_conv, state_mamba_ssm,
              norm1_w, w_in, rwkv_mu, rwkv_w0, rwkv_w2, rwkv_a0, rwkv_a2, rwkv_g2, rwkv_k_k, rwkv_k_a,
              rwkv_r_k, rwkv_ln_w, rwkv_ln_b, mamba_conv_w, mamba_conv_b, mamba_dt_bias, mamba_A_log, mamba_D,
              mamba_norm_w, w_out, norm2_w, peer_w_q, peer_sub_keys, peer_u, peer_v, final_norm_w):
    layer_weights = (norm1_w, w_in, rwkv_mu, rwkv_w0, rwkv_w2, rwkv_a0, rwkv_a2, rwkv_g2, rwkv_k_k, rwkv_k_a,
                     rwkv_r_k, rwkv_ln_w, rwkv_ln_b, mamba_conv_w, mamba_conv_b, mamba_dt_bias, mamba_A_log,
                     mamba_D, mamba_norm_w, w_out, norm2_w, peer_w_q, peer_sub_keys, peer_u, peer_v)
    bp = x_prompt.shape[0]
    dtp = x_prompt.dtype
    z_shift = jnp.zeros((bp, R_PROJ), dtp)
    z_wkv = jnp.zeros((bp, R_HEADS, R_HEAD, R_HEAD), dtp)
    z_conv = jnp.zeros((bp, CONV_W - 1, CONV_DIM), dtp)
    z_ssm = jnp.zeros((bp, M_HEADS, M_HEADDIM, M_STATE), dtp)
    hp, hs = x_prompt, x_sample
    new_p = ([], [], [], [])
    new_s = ([], [], [], [])
    for l in range(DEPTH):
        lw = [w[l] for w in layer_weights]
        hp, *sp = trunk_layer(hp, z_shift, z_wkv, z_conv, z_ssm, *lw)
        hs, *ss = trunk_layer(hs, state_rwkv_shift[l], state_rwkv_wkv[l], state_mamba_conv[l],
                              state_mamba_ssm[l], *lw)
        for acc, val in zip(new_p, sp):
            acc.append(val)
        for acc, val in zip(new_s, ss):
            acc.append(val)
    y_prompt = rmsnorm(hp, final_norm_w)
    y_sample = rmsnorm(hs, final_norm_w)
    return (y_prompt, y_sample,
            jnp.stack(new_p[0]), jnp.stack(new_p[1]), jnp.stack(new_p[2]), jnp.stack(new_p[3]),
            jnp.stack(new_s[0]), jnp.stack(new_s[1]), jnp.stack(new_s[2]), jnp.stack(new_s[3]))
```

```python
import functools
import math

import jax
import jax.numpy as jnp
import numpy as np
from jax import lax
from jax.experimental import pallas as pl
from jax.experimental.pallas import tpu as pltpu

LANES = 128
SUBLANES = 8
VMEM_BYTES_V7X = 64 * 1024 * 1024

D_MODEL = 1024
CHUNK = 64
R_WIDTH = 512
R_HEAD = 64
R_HEADS = R_WIDTH // R_HEAD
DECAY_LORA = 64
AAA_LORA = 64
GATE_LORA = 128
R_PROJ = 3 * R_WIDTH + DECAY_LORA + AAA_LORA + GATE_LORA
R_GN_EPS = R_HEAD * 1e-5
M_WIDTH = 512
M_HEADDIM = 64
M_HEADS = M_WIDTH // M_HEADDIM
M_GROUPS = 2
M_STATE = 64
CONV_W = 4
CONV_DIM = M_WIDTH + 2 * M_GROUPS * M_STATE
M_PROJ = M_WIDTH + CONV_DIM + M_HEADS
M_PROJ_PAD = M_WIDTH + CONV_DIM + LANES
N_KEYS = 128
N_EXPERTS = N_KEYS * N_KEYS
P_HEADS = 8
P_QDIM = 256
P_TOPK = 16
PAIRS = P_HEADS * P_TOPK
NORM_EPS = 1e-6

HI = lax.Precision.HIGHEST
F32 = jnp.float32
BF16 = jnp.bfloat16

TAB_ROWS_PER_EXPERT = 4
TAB_PAD = 8
TAB_ROWS = N_EXPERTS * TAB_ROWS_PER_EXPERT + 2 * TAB_PAD


def _dot(a, b, precision=HI):
    return jnp.dot(a, b, precision=precision, preferred_element_type=F32)


def _dot_nt(a, b, precision=HI):
    return lax.dot_general(a, b, (((1,), (1,)), ((), ())), precision=precision, preferred_element_type=F32)


def _dot_tn(a, b, precision=HI):
    return lax.dot_general(a, b, (((0,), (0,)), ((), ())), precision=precision, preferred_element_type=F32)


def _softplus(x):
    return jnp.maximum(x, 0.0) + jnp.log(1.0 + jnp.exp(-jnp.abs(x)))


def _sigmoid(x):
    return 1.0 / (1.0 + jnp.exp(-x))


def _silu(x):
    return x * _sigmoid(x)


def _in_proj_kernel(x_ref, nw_ref, w_ref, pr_ref, pm_ref):
    x = x_ref[...]
    xn = x * lax.rsqrt(jnp.mean(x * x, axis=-1, keepdims=True) + NORM_EPS) * nw_ref[...]
    p = jnp.dot(xn.astype(BF16), w_ref[...], preferred_element_type=F32)
    pr_ref[...] = p[:, :R_PROJ]
    pm_ref[...] = p[:, R_PROJ:]


def _in_proj(x2d, norm_w, w_cat_bf16, tm=512):
    n = x2d.shape[0]
    tm = math.gcd(tm, n)
    wtot = R_PROJ + M_PROJ_PAD
    return pl.pallas_call(
        _in_proj_kernel,
        grid=(n // tm,),
        in_specs=[
            pl.BlockSpec((tm, D_MODEL), lambda i: (i, 0)),
            pl.BlockSpec((1, D_MODEL), lambda i: (0, 0)),
            pl.BlockSpec((D_MODEL, wtot), lambda i: (0, 0)),
        ],
        out_specs=[
            pl.BlockSpec((tm, R_PROJ), lambda i: (i, 0)),
            pl.BlockSpec((tm, M_PROJ_PAD), lambda i: (i, 0)),
        ],
        out_shape=[
            jax.ShapeDtypeStruct((n, R_PROJ), F32),
            jax.ShapeDtypeStruct((n, M_PROJ_PAD), F32),
        ],
        compiler_params=pltpu.CompilerParams(
            dimension_semantics=("arbitrary",), vmem_limit_bytes=48 * 1024 * 1024),
        name="in_proj",
    )(x2d, norm_w, w_cat_bf16)


def _rwkv_kernel(p_ref, shift0_ref, wkv0_ref, mu_ref, w0_ref, w2_ref, a0_ref, a2_ref, g2_ref, kk_ref, ka_ref,
                 rk_ref, lnw_ref, lnb_ref, o_ref, wkv_out_ref, s_scr, prev_scr):
    c = pl.program_id(1)
    nc = pl.num_programs(1)
    L = CHUNK

    @pl.when(c == 0)
    def _():
        prev_scr[...] = jnp.broadcast_to(shift0_ref[0], prev_scr.shape)
        s_scr[...] = wkv0_ref[0]

    p = p_ref[0]
    row = lax.broadcasted_iota(jnp.int32, (L, 1), 0)
    p_prev = jnp.where(row == 0, prev_scr[0:1, :], pltpu.roll(p, 1, 0))
    prev_scr[...] = jnp.broadcast_to(p[L - 1:L, :], prev_scr.shape)
    pm = p + (p_prev - p) * mu_ref[...]
    r = pm[:, 0:R_WIDTH]
    k = pm[:, R_WIDTH:2 * R_WIDTH]
    v = pm[:, 2 * R_WIDTH:3 * R_WIDTH]
    wa = pm[:, 3 * R_WIDTH:3 * R_WIDTH + LANES]
    gd = pm[:, 3 * R_WIDTH + LANES:3 * R_WIDTH + 2 * LANES]

    w_log = -_softplus(-(w0_ref[...] + _dot(jnp.tanh(wa), w2_ref[...]))) - 0.5
    lw = -jnp.exp(w_log)
    a = _sigmoid(a0_ref[...] + _dot(wa, a2_ref[...]))
    g = _dot(_sigmoid(gd), g2_ref[...])

    li = lax.broadcasted_iota(jnp.int32, (R_WIDTH, R_WIDTH), 0) // R_HEAD
    lj = lax.broadcasted_iota(jnp.int32, (R_WIDTH, R_WIDTH), 1) // R_HEAD
    seg = (li == lj).astype(F32)

    kk = k * kk_ref[...]
    kk = kk / jnp.maximum(jnp.sqrt(_dot(kk * kk, seg)), 1e-12)
    k_mod = k * (1.0 + (a - 1.0) * ka_ref[...])
    a_vec = -kk
    b_vec = kk * a

    ti = lax.broadcasted_iota(jnp.int32, (L, L), 0)
    si = lax.broadcasted_iota(jnp.int32, (L, L), 1)
    tri_incl = (si <= ti)
    tri_strict = (si < ti)
    cl = _dot(tri_incl.astype(F32), lw)
    e_in = jnp.exp(cl)
    e_prev = jnp.exp(cl - lw)
    e_neg = jnp.exp(-cl)
    At = a_vec * e_prev
    Bt = b_vec * e_neg
    Kt = k_mod * e_neg
    Rt = r * e_in
    eye = (ti == si).astype(F32)

    outs = []
    for h in range(R_HEADS):
        sl = slice(h * R_HEAD, (h + 1) * R_HEAD)
        Ah, Bh, Kh, Rh, Vh = At[:, sl], Bt[:, sl], Kt[:, sl], Rt[:, sl], v[:, sl]
        Sh = s_scr[h]
        AB = jnp.where(tri_strict, _dot_nt(Ah, Bh), 0.0)
        AK = jnp.where(tri_strict, _dot_nt(Ah, Kh), 0.0)
        RB = jnp.where(tri_incl, _dot_nt(Rh, Bh), 0.0)
        RK = jnp.where(tri_incl, _dot_nt(Rh, Kh), 0.0)
        T = eye + AB
        Np = AB
        for _ in range(5):
            Np = _dot(Np, Np)
            T = T + _dot(T, Np)
        U = _dot(T, _dot_nt(Ah, Sh) + _dot(AK, Vh))
        O = _dot_nt(Rh, Sh) + _dot(RB, U) + _dot(RK, Vh)
        s_new = (Sh + _dot_tn(U, Bh) + _dot_tn(Vh, Kh)) * e_in[L - 1:L, sl]
        s_scr[h] = s_new
        mean = jnp.mean(O, axis=-1, keepdims=True)
        var = jnp.mean(jnp.square(O - mean), axis=-1, keepdims=True)
        outs.append((O - mean) * lax.rsqrt(var + R_GN_EPS))
    o = jnp.concatenate(outs, axis=-1)
    bonus = _dot(r * k_mod * rk_ref[...], seg) * v
    o_ref[0] = (o * lnw_ref[...] + lnb_ref[...] + bonus) * g

    @pl.when(c == nc - 1)
    def _():
        wkv_out_ref[0] = s_scr[...]


def _rwkv(proj_r, shift0, wkv0, wts):
    bsz, t, _ = proj_r.shape
    nc = t // CHUNK
    full = lambda shape: pl.BlockSpec(shape, lambda b, c: (0,) * len(shape))
    return pl.pallas_call(
        _rwkv_kernel,
        grid=(bsz, nc),
        in_specs=[
            pl.BlockSpec((1, CHUNK, R_PROJ), lambda b, c: (b, c, 0)),
            pl.BlockSpec((1, 1, R_PROJ), lambda b, c: (b, 0, 0)),
            pl.BlockSpec((1, R_HEADS, R_HEAD, R_HEAD), lambda b, c: (b, 0, 0, 0)),
            full((1, R_PROJ)), full((1, R_WIDTH)), full((LANES, R_WIDTH)), full((1, R_WIDTH)),
            full((LANES, R_WIDTH)), full((GATE_LORA, R_WIDTH)), full((1, R_WIDTH)), full((1, R_WIDTH)),
            full((1, R_WIDTH)), full((1, R_WIDTH)), full((1, R_WIDTH)),
        ],
        out_specs=[
            pl.BlockSpec((1, CHUNK, R_WIDTH), lambda b, c: (b, c, 0)),
            pl.BlockSpec((1, R_HEADS, R_HEAD, R_HEAD), lambda b, c: (b, 0, 0, 0)),
        ],
        out_shape=[
            jax.ShapeDtypeStruct((bsz, t, R_WIDTH), F32),
            jax.ShapeDtypeStruct((bsz, R_HEADS, R_HEAD, R_HEAD), F32),
        ],
        scratch_shapes=[pltpu.VMEM((R_HEADS, R_HEAD, R_HEAD), F32), pltpu.VMEM((SUBLANES, R_PROJ), F32)],
        compiler_params=pltpu.CompilerParams(
            dimension_semantics=("arbitrary", "arbitrary"), vmem_limit_bytes=40 * 1024 * 1024),
        name="rwkv",
    )(proj_r, shift0.reshape(bsz, 1, R_PROJ), wkv0, *wts)


def _ssd_kernel(p_ref, conv0_ref, ssm0_ref, cw_ref, cb_ref, dtb_ref, alog_ref, dvec_ref, nw_ref,
                o_ref, conv_out_ref, ssm_out_ref, h_scr, prev_scr):
    c = pl.program_id(1)
    nc = pl.num_programs(1)
    L = CHUNK

    @pl.when(c == 0)
    def _():
        prev_scr[...] = conv0_ref[0]
        h_scr[...] = ssm0_ref[0]

    p = p_ref[0]
    z = p[:, 0:M_WIDTH]
    xbc = p[:, M_WIDTH:M_WIDTH + CONV_DIM]
    dt_raw = p[:, M_WIDTH + CONV_DIM:]
    prev8 = prev_scr[...]
    row8 = lax.broadcasted_iota(jnp.int32, (SUBLANES, 1), 0)
    conv = xbc * cw_ref[CONV_W - 1:CONV_W, :]
    for j in range(1, CONV_W):
        xs_j = pltpu.roll(xbc, j, 0)
        first = jnp.where(row8 >= j, xs_j[0:SUBLANES], pltpu.roll(prev8, j, 0))
        shifted = jnp.concatenate([first, xs_j[SUBLANES:]], axis=0)
        conv = conv + shifted * cw_ref[CONV_W - 1 - j:CONV_W - j, :]
    prev_scr[...] = xbc[L - SUBLANES:L, :]
    act = _silu(conv + cb_ref[...])
    xs = act[:, 0:M_WIDTH]
    Bm = act[:, M_WIDTH:M_WIDTH + M_GROUPS * M_STATE]
    Cm = act[:, M_WIDTH + M_GROUPS * M_STATE:]
    dt = _softplus(dt_raw + dtb_ref[...])
    A = -jnp.exp(alog_ref[...])
    ti = lax.broadcasted_iota(jnp.int32, (L, L), 0)
    si = lax.broadcasted_iota(jnp.int32, (L, L), 1)
    causal = si <= ti
    cs = _dot(causal.astype(F32), dt * A)
    cs_t = cs.T
    ys = []
    for h in range(M_HEADS):
        g = h // (M_HEADS // M_GROUPS)
        sl = slice(h * M_HEADDIM, (h + 1) * M_HEADDIM)
        gs = slice(g * M_STATE, (g + 1) * M_STATE)
        xh = xs[:, sl]
        Bg = Bm[:, gs]
        Cg = Cm[:, gs]
        cs_col = cs[:, h:h + 1]
        cs_row = cs_t[h:h + 1, :]
        decay = jnp.where(causal, jnp.exp(jnp.where(causal, cs_col - cs_row, 0.0)), 0.0)
        xdt = xh * dt[:, h:h + 1]
        cb = _dot_nt(Cg, Bg)
        hin = h_scr[h]
        y = _dot(cb * decay, xdt) + jnp.exp(cs_col) * _dot_nt(Cg, hin)
        cs_last = cs[L - 1:L, h:h + 1]
        to_end = jnp.exp(cs_last - cs_col)
        h_scr[h] = hin * jnp.exp(cs_last) + _dot_tn(xdt * to_end, Bg)
        ys.append(y + dvec_ref[:, h:h + 1] * xh)
    y = jnp.concatenate(ys, axis=-1) * _silu(z)
    gw = M_WIDTH // M_GROUPS
    parts = []
    for g in range(M_GROUPS):
        yg = y[:, g * gw:(g + 1) * gw]
        parts.append(yg * lax.rsqrt(jnp.mean(yg * yg, axis=-1, keepdims=True) + NORM_EPS))
    o_ref[0] = jnp.concatenate(parts, axis=-1) * nw_ref[...]

    @pl.when(c == nc - 1)
    def _():
        conv_out_ref[0] = prev_scr[...]
        ssm_out_ref[0] = h_scr[...]


def _ssd(proj_m, conv0_8, ssm0, wts):
    bsz, t, _ = proj_m.shape
    nc = t // CHUNK
    full = lambda shape: pl.BlockSpec(shape, lambda b, c: (0,) * len(shape))
    return pl.pallas_call(
        _ssd_kernel,
        grid=(bsz, nc),
        in_specs=[
            pl.BlockSpec((1, CHUNK, M_PROJ_PAD), lambda b, c: (b, c, 0)),
            pl.BlockSpec((1, SUBLANES, CONV_DIM), lambda b, c: (b, 0, 0)),
            pl.BlockSpec((1, M_HEADS, M_HEADDIM, M_STATE), lambda b, c: (b, 0, 0, 0)),
            full((CONV_W, CONV_DIM)), full((1, CONV_DIM)), full((1, LANES)), full((1, LANES)),
            full((1, LANES)), full((1, M_WIDTH)),
        ],
        out_specs=[
            pl.BlockSpec((1, CHUNK, M_WIDTH), lambda b, c: (b, c, 0)),
            pl.BlockSpec((1, SUBLANES, CONV_DIM), lambda b, c: (b, 0, 0)),
            pl.BlockSpec((1, M_HEADS, M_HEADDIM, M_STATE), lambda b, c: (b, 0, 0, 0)),
        ],
        out_shape=[
            jax.ShapeDtypeStruct((bsz, t, M_WIDTH), F32),
            jax.ShapeDtypeStruct((bsz, SUBLANES, CONV_DIM), F32),
            jax.ShapeDtypeStruct((bsz, M_HEADS, M_HEADDIM, M_STATE), F32),
        ],
        scratch_shapes=[pltpu.VMEM((M_HEADS, M_HEADDIM, M_STATE), F32), pltpu.VMEM((SUBLANES, CONV_DIM), F32)],
        compiler_params=pltpu.CompilerParams(
            dimension_semantics=("arbitrary", "arbitrary"), vmem_limit_bytes=40 * 1024 * 1024),
        name="ssd",
    )(proj_m, conv0_8, ssm0, *wts)


def _top16_rows(s, rowid, big):
    vals, ids = [], []
    for _ in range(P_TOPK):
        m = jnp.max(s, axis=0, keepdims=True)
        idx = jnp.min(jnp.where(s == m, rowid, big), axis=0, keepdims=True)
        s = jnp.where(rowid == idx, -jnp.inf, s)
        vals.append(m)
        ids.append(idx)
    return vals, ids


def _route_kernel(or_ref, om_ref, x_ref, wr_ref, wm_ref, n2_ref, wq_ref, keys_ref,
                  h_ref, xn_ref, off_ref, gate_ref, sv_scr, si_scr):
    tm = x_ref.shape[0]
    h = (x_ref[...]
         + jnp.dot(or_ref[...].astype(BF16), wr_ref[...], preferred_element_type=F32)
         + jnp.dot(om_ref[...].astype(BF16), wm_ref[...], preferred_element_type=F32))
    h_ref[...] = h
    xn = h * lax.rsqrt(jnp.mean(h * h, axis=-1, keepdims=True) + NORM_EPS) * n2_ref[...]
    xn_ref[...] = xn
    xn_b = xn.astype(BF16)
    rowid = lax.broadcasted_iota(jnp.int32, (N_KEYS, tm), 0)

    def sub_body(j, carry):
        q = jnp.dot(xn_b, wq_ref[j], preferred_element_type=F32)
        s_t = _dot_nt(keys_ref[j % 2], q)
        vals, ids = _top16_rows(s_t, rowid, N_KEYS)
        sv_scr[j] = jnp.concatenate(vals, axis=0)
        si_scr[j] = jnp.concatenate(ids, axis=0)
        return carry

    lax.fori_loop(0, 2 * P_HEADS, sub_body, 0)

    r16 = lax.broadcasted_iota(jnp.int32, (16, 1), 0)
    r8 = lax.broadcasted_iota(jnp.int32, (8, 1), 0)

    def head_body(hd, carry):
        sv0, sv1 = sv_scr[2 * hd], sv_scr[2 * hd + 1]
        si0, si1 = si_scr[2 * hd], si_scr[2 * hd + 1]
        cand, cid, flat = [], [], []
        for i in range(8):
            n_i = P_TOPK // (i + 1)
            rows = 16 if i == 0 else 8
            rr = r16 if i == 0 else r8
            cand.append(jnp.where(rr < n_i, sv0[i:i + 1] + sv1[0:rows], -jnp.inf))
            cid.append(si0[i:i + 1] * N_KEYS + si1[0:rows])
            flat.append(jnp.broadcast_to(i * P_TOPK + rr, (rows, tm)))
        cand.append(sv0[8:16] + sv1[0:1])
        cid.append(si0[8:16] * N_KEYS + si1[0:1])
        flat.append(jnp.broadcast_to((8 + r8) * P_TOPK, (8, tm)))
        cand = jnp.concatenate(cand, axis=0)
        cid = jnp.concatenate(cid, axis=0)
        flat = jnp.concatenate(flat, axis=0)
        tops, eids = [], []
        for _ in range(P_TOPK):
            m = jnp.max(cand, axis=0, keepdims=True)
            sel = jnp.min(jnp.where(cand == m, flat, 4 * P_TOPK * P_TOPK), axis=0, keepdims=True)
            hit = flat == sel
            eids.append(jnp.sum(jnp.where(hit, cid, 0), axis=0, keepdims=True))
            cand = jnp.where(hit, -jnp.inf, cand)
            tops.append(m)
        top_s = jnp.concatenate(tops, axis=0)
        eid = jnp.concatenate(eids, axis=0)
        ex = jnp.exp(top_s - top_s[0:1])
        gate = ex / jnp.sum(ex, axis=0, keepdims=True)
        prow = lax.broadcasted_iota(jnp.int32, (P_TOPK, 1), 0)
        upper = ((prow % 8) >= 4).astype(jnp.int32)
        si_scr[hd] = TAB_PAD + TAB_ROWS_PER_EXPERT * eid - TAB_ROWS_PER_EXPERT * upper
        sv_scr[hd] = gate
        return carry

    lax.fori_loop(0, P_HEADS, head_body, 0)
    off_ref[...] = jnp.concatenate([si_scr[hd] for hd in range(P_HEADS)], axis=0).T
    gate_ref[...] = jnp.concatenate([sv_scr[hd] for hd in range(P_HEADS)], axis=0).T


def _route(o_r, o_m, x2d, w_out_r, w_out_m, norm2_w, wq3, keys, tm=256):
    n = x2d.shape[0]
    tm = math.gcd(tm, n)
    full = lambda shape: pl.BlockSpec(shape, lambda i: (0,) * len(shape))
    return pl.pallas_call(
        _route_kernel,
        grid=(n // tm,),
        in_specs=[
            pl.BlockSpec((tm, R_WIDTH), lambda i: (i, 0)),
            pl.BlockSpec((tm, M_WIDTH), lambda i: (i, 0)),
            pl.BlockSpec((tm, D_MODEL), lambda i: (i, 0)),
            full((R_WIDTH, D_MODEL)), full((M_WIDTH, D_MODEL)), full((1, D_MODEL)),
            full((2 * P_HEADS, D_MODEL, P_QDIM // 2)), full((2, N_KEYS, P_QDIM // 2)),
        ],
        out_specs=[
            pl.BlockSpec((tm, D_MODEL), lambda i: (i, 0)),
            pl.BlockSpec((tm, D_MODEL), lambda i: (i, 0)),
            pl.BlockSpec((tm, PAIRS), lambda i: (i, 0)),
            pl.BlockSpec((tm, PAIRS), lambda i: (i, 0)),
        ],
        out_shape=[
            jax.ShapeDtypeStruct((n, D_MODEL), F32),
            jax.ShapeDtypeStruct((n, D_MODEL), F32),
            jax.ShapeDtypeStruct((n, PAIRS), jnp.int32),
            jax.ShapeDtypeStruct((n, PAIRS), F32),
        ],
        scratch_shapes=[pltpu.VMEM((2 * P_HEADS, P_TOPK, tm), F32), pltpu.VMEM((2 * P_HEADS, P_TOPK, tm), jnp.int32)],
        compiler_params=pltpu.CompilerParams(
            dimension_semantics=("arbitrary",), vmem_limit_bytes=48 * 1024 * 1024),
        name="route",
    )(o_r, o_m, x2d, w_out_r, w_out_m, norm2_w, wq3, keys)


def _pack_table(tab):
    tb = tab.astype(BF16)
    half = D_MODEL // 2
    lo = lax.bitcast_convert_type(tb[:, :half], jnp.uint16).astype(jnp.uint32)
    hi = lax.bitcast_convert_type(tb[:, half:], jnp.uint16).astype(jnp.uint32)
    w = (lo | (hi << 16)).reshape(N_EXPERTS * TAB_ROWS_PER_EXPERT, LANES)
    return jnp.pad(w, ((TAB_PAD, TAB_PAD), (0, 0)))


def _merged_tile(tab_vmem, off_ref, t, pa, pb, low):
    wa = tab_vmem[pl.ds(off_ref[t, pa], SUBLANES), :]
    wb = tab_vmem[pl.ds(off_ref[t, pb], SUBLANES), :]
    return jnp.where(low, wa, wb)


def _unpack_words(w):
    lo = pltpu.bitcast(w << 16, F32)
    hi = pltpu.bitcast(w & jnp.uint32(0xFFFF0000), F32)
    return lo, hi


def _peer_u_kernel(off_ref, x3_ref, gate_ref, tab_vmem, c_ref, r_scr):
    tb = x3_ref.shape[0]
    row = lax.broadcasted_iota(jnp.int32, (SUBLANES, LANES), 0)
    low = row < 4
    m2 = (row % 4) < 2
    m1 = (row % 2) == 0
    ones = jnp.ones((SUBLANES, LANES), F32)

    def fold(A, B, sh, m):
        return jnp.where(m, A + pltpu.roll(A, SUBLANES - sh, 0), B + pltpu.roll(B, sh, 0))

    def group_body(tg, carry):
        for j in range(SUBLANES):
            t = tg * SUBLANES + j
            x = x3_ref[t]
            xr = pltpu.roll(x, 4, 0)
            xa = jnp.where(low, x, xr)
            xb = jnp.where(low, xr, x)

            def prod(pa, pb):
                lo, hi = _unpack_words(_merged_tile(tab_vmem, off_ref, t, pa, pb, low))
                return lo * xa + hi * xb

            for g in range(PAIRS // 8):
                b = g * 8
                n1 = fold(prod(b + 0, b + 4), prod(b + 2, b + 6), 2, m2)
                n2 = fold(prod(b + 1, b + 5), prod(b + 3, b + 7), 2, m2)
                r_scr[j, pl.ds(b, 8), :] = fold(n1, n2, 1, m1)
        rows = [_dot_nt(ones, r_scr[j])[0:1, :] for j in range(SUBLANES)]
        hid = jnp.concatenate(rows, axis=0)
        sl = pl.ds(pl.multiple_of(tg * SUBLANES, SUBLANES), SUBLANES)
        gelu = 0.5 * hid * (1.0 + lax.erf(hid * (1.0 / math.sqrt(2.0))))
        c_ref[sl, :] = gelu * gate_ref[sl, :]
        return carry

    lax.fori_loop(0, tb // SUBLANES, group_body, 0)


def _peer_u(off, x3, gate, tab, tb=128):
    n = off.shape[0]
    tb = math.gcd(tb, n)
    return pl.pallas_call(
        _peer_u_kernel,
        grid=(n // tb,),
        in_specs=[
            pl.BlockSpec((tb, PAIRS), lambda i: (i, 0), memory_space=pltpu.SMEM),
            pl.BlockSpec((tb, SUBLANES, LANES), lambda i: (i, 0, 0)),
            pl.BlockSpec((tb, PAIRS), lambda i: (i, 0)),
            pl.BlockSpec(memory_space=pltpu.VMEM),
        ],
        out_specs=pl.BlockSpec((tb, PAIRS), lambda i: (i, 0)),
        out_shape=jax.ShapeDtypeStruct((n, PAIRS), F32),
        scratch_shapes=[pltpu.VMEM((SUBLANES, PAIRS, LANES), F32)],
        compiler_params=pltpu.CompilerParams(
            dimension_semantics=("arbitrary",), vmem_limit_bytes=56 * 1024 * 1024),
        name="peer_u",
    )(off, x3, gate, tab)


def _peer_v_kernel(off_ref, c_ref, h3_ref, fw_ref, tab_vmem, y_ref):
    tb = h3_ref.shape[0]
    row = lax.broadcasted_iota(jnp.int32, (SUBLANES, LANES), 0)
    low = row < 4
    fw = fw_ref[...]

    def token_body(t, carry):
        acc_lo = jnp.zeros((SUBLANES, LANES), F32)
        acc_hi = jnp.zeros((SUBLANES, LANES), F32)
        for g in range(PAIRS // 8):
            for (pa, pb) in ((0, 4), (1, 5), (2, 6), (3, 7)):
                pa, pb = g * 8 + pa, g * 8 + pb
                lo, hi = _unpack_words(_merged_tile(tab_vmem, off_ref, t, pa, pb, low))
                cv = jnp.where(low, c_ref[t, pa], c_ref[t, pb])
                acc_lo = acc_lo + cv * lo
                acc_hi = acc_hi + cv * hi
        lo_sum = acc_lo + pltpu.roll(acc_lo, 4, 0)
        hi_sum = acc_hi + pltpu.roll(acc_hi, 4, 0)
        hn = h3_ref[t] + jnp.where(low, lo_sum, hi_sum)
        ms = jnp.sum(jnp.sum(hn * hn, axis=1, keepdims=True), axis=0, keepdims=True) * (1.0 / D_MODEL)
        y_ref[t] = hn * lax.rsqrt(ms + NORM_EPS) * fw
        return carry

    lax.fori_loop(0, tb, token_body, 0)


def _peer_v(off, c, h3, fw, tab, tb=128):
    n = off.shape[0]
    tb = math.gcd(tb, n)
    return pl.pallas_call(
        _peer_v_kernel,
        grid=(n // tb,),
        in_specs=[
            pl.BlockSpec((tb, PAIRS), lambda i: (i, 0), memory_space=pltpu.SMEM),
            pl.BlockSpec((tb, PAIRS), lambda i: (i, 0), memory_space=pltpu.SMEM),
            pl.BlockSpec((tb, SUBLANES, LANES), lambda i: (i, 0, 0)),
            pl.BlockSpec((SUBLANES, LANES), lambda i: (0, 0)),
            pl.BlockSpec(memory_space=pltpu.VMEM),
        ],
        out_specs=pl.BlockSpec((tb, SUBLANES, LANES), lambda i: (i, 0, 0)),
        out_shape=jax.ShapeDtypeStruct((n, SUBLANES, LANES), F32),
        compiler_params=pltpu.CompilerParams(
            dimension_semantics=("arbitrary",), vmem_limit_bytes=56 * 1024 * 1024),
        name="peer_v",
    )(off, c, h3, fw, tab)


def _prep_weights(norm1_w, w_in, rwkv_mu, rwkv_w0, rwkv_w2, rwkv_a0, rwkv_a2, rwkv_g2, rwkv_k_k, rwkv_k_a,
                  rwkv_r_k, rwkv_ln_w, rwkv_ln_b, mamba_conv_w, mamba_conv_b, mamba_dt_bias, mamba_A_log,
                  mamba_D, mamba_norm_w, w_out, norm2_w, peer_w_q, peer_sub_keys, peer_u, peer_v, final_norm_w):
    row = lambda a: a.reshape(1, -1)
    pad_lanes = lambda a: jnp.pad(row(a), ((0, 0), (0, LANES - a.shape[-1])))
    w_cat = jnp.pad(w_in, ((0, 0), (0, M_PROJ_PAD - M_PROJ))).astype(BF16)
    zeros = jnp.zeros((DECAY_LORA, R_WIDTH), F32)
    rw = (row(rwkv_mu), row(rwkv_w0), jnp.concatenate([rwkv_w2, zeros], 0), row(rwkv_a0),
          jnp.concatenate([zeros, rwkv_a2], 0), rwkv_g2, row(rwkv_k_k), row(rwkv_k_a), row(rwkv_r_k),
          row(rwkv_ln_w), row(rwkv_ln_b))
    mw = (mamba_conv_w, row(mamba_conv_b), pad_lanes(mamba_dt_bias), pad_lanes(mamba_A_log),
          pad_lanes(mamba_D), row(mamba_norm_w))
    wq3 = peer_w_q.reshape(D_MODEL, 2 * P_HEADS, P_QDIM // 2).transpose(1, 0, 2).astype(BF16)
    route_w = (w_out[:R_WIDTH].astype(BF16), w_out[R_WIDTH:].astype(BF16), row(norm2_w), wq3, peer_sub_keys)
    return dict(norm1=row(norm1_w), w_cat=w_cat, rw=rw, mw=mw, route_w=route_w,
                tab_u=_pack_table(peer_u), tab_v=_pack_table(peer_v),
                fw=final_norm_w.reshape(SUBLANES, LANES))


def _trunk(x, shift0, wkv0, conv0, ssm0, w):
    bsz, t, _ = x.shape
    n = bsz * t
    x2d = x.reshape(n, D_MODEL)
    proj_r, proj_m = _in_proj(x2d, w["norm1"], w["w_cat"])
    proj_r = proj_r.reshape(bsz, t, R_PROJ)
    proj_m = proj_m.reshape(bsz, t, M_PROJ_PAD)
    o_r, new_wkv = _rwkv(proj_r, shift0, wkv0, w["rw"])
    conv0_8 = jnp.pad(conv0, ((0, 0), (SUBLANES - (CONV_W - 1), 0), (0, 0)))
    o_m, new_conv8, new_ssm = _ssd(proj_m, conv0_8, ssm0, w["mw"])
    h, xn, off, gate = _route(o_r.reshape(n, R_WIDTH), o_m.reshape(n, M_WIDTH), x2d, *w["route_w"])
    c = _peer_u(off, xn.reshape(n, SUBLANES, LANES), gate, w["tab_u"])
    y = _peer_v(off, c, h.reshape(n, SUBLANES, LANES), w["fw"], w["tab_v"])
    new_shift = proj_r[:, -1, :]
    new_conv = new_conv8[:, SUBLANES - (CONV_W - 1):, :]
    return y.reshape(bsz, t, D_MODEL), new_shift[None], new_wkv[None], new_conv[None], new_ssm[None]


def kernel(x_prompt, x_sample, state_rwkv_shift, state_rwkv_wkv, state_mamba_conv, state_mamba_ssm, norm1_w, w_in, rwkv_mu, rwkv_w0, rwkv_w2, rwkv_a0, rwkv_a2, rwkv_g2, rwkv_k_k, rwkv_k_a, rwkv_r_k, rwkv_ln_w, rwkv_ln_b, mamba_conv_w, mamba_conv_b, mamba_dt_bias, mamba_A_log, mamba_D, mamba_norm_w, w_out, norm2_w, peer_w_q, peer_sub_keys, peer_u, peer_v, final_norm_w):
    w = _prep_weights(norm1_w[0], w_in[0], rwkv_mu[0], rwkv_w0[0], rwkv_w2[0], rwkv_a0[0], rwkv_a2[0], rwkv_g2[0],
                      rwkv_k_k[0], rwkv_k_a[0], rwkv_r_k[0].reshape(-1), rwkv_ln_w[0], rwkv_ln_b[0],
                      mamba_conv_w[0], mamba_conv_b[0], mamba_dt_bias[0], mamba_A_log[0], mamba_D[0],
                      mamba_norm_w[0], w_out[0], norm2_w[0], peer_w_q[0], peer_sub_keys[0], peer_u[0], peer_v[0],
                      final_norm_w)
    bp = x_prompt.shape[0]
    zp = lambda *s: jnp.zeros((bp,) + s, F32)
    yp, *sp = _trunk(x_prompt, zp(R_PROJ), zp(R_HEADS, R_HEAD, R_HEAD), zp(CONV_W - 1, CONV_DIM),
                     zp(M_HEADS, M_HEADDIM, M_STATE), w)
    ys, *ss = _trunk(x_sample, state_rwkv_shift[0], state_rwkv_wkv[0], state_mamba_conv[0], state_mamba_ssm[0], w)
    return (yp, ys, *sp, *ss)
```

```python
import functools
import math

import jax
import jax.numpy as jnp
import numpy as np
from jax import lax
from jax.experimental import pallas as pl
from jax.experimental.pallas import tpu as pltpu

LANES = 128
SUBLANES = 8
VMEM_BYTES_V7X = 64 * 1024 * 1024

D_MODEL = 1024
CHUNK = 64
R_WIDTH = 512
R_HEAD = 64
R_HEADS = R_WIDTH // R_HEAD
DECAY_LORA = 64
AAA_LORA = 64
GATE_LORA = 128
R_PROJ = 3 * R_WIDTH + DECAY_LORA + AAA_LORA + GATE_LORA
R_GN_EPS = R_HEAD * 1e-5
M_WIDTH = 512
M_HEADDIM = 64
M_HEADS = M_WIDTH // M_HEADDIM
M_GROUPS = 2
M_STATE = 64
CONV_W = 4
CONV_DIM = M_WIDTH + 2 * M_GROUPS * M_STATE
M_PROJ = M_WIDTH + CONV_DIM + M_HEADS
M_PROJ_PAD = M_WIDTH + CONV_DIM + LANES
N_KEYS = 128
N_EXPERTS = N_KEYS * N_KEYS
P_HEADS = 8
P_QDIM = 256
P_TOPK = 16
PAIRS = P_HEADS * P_TOPK
NORM_EPS = 1e-6

HI = lax.Precision.HIGHEST
F32 = jnp.float32
BF16 = jnp.bfloat16

TAB_ROWS_PER_EXPERT = 4
TAB_PAD = 8
TAB_ROWS = N_EXPERTS * TAB_ROWS_PER_EXPERT + 2 * TAB_PAD


def _dot(a, b, precision=HI):
    return jnp.dot(a, b, precision=precision, preferred_element_type=F32)


def _dot_nt(a, b, precision=HI):
    return lax.dot_general(a, b, (((1,), (1,)), ((), ())), precision=precision, preferred_element_type=F32)


def _dot_tn(a, b, precision=HI):
    return lax.dot_general(a, b, (((0,), (0,)), ((), ())), precision=precision, preferred_element_type=F32)


def _softplus(x):
    return jnp.maximum(x, 0.0) + jnp.log(1.0 + jnp.exp(-jnp.abs(x)))


def _sigmoid(x):
    return 1.0 / (1.0 + jnp.exp(-x))


def _silu(x):
    return x * _sigmoid(x)


def _in_proj_kernel(x_ref, nw_ref, w_ref, pr_ref, pm_ref):
    x = x_ref[...]
    xn = x * lax.rsqrt(jnp.mean(x * x, axis=-1, keepdims=True) + NORM_EPS) * nw_ref[...]
    p = jnp.dot(xn.astype(BF16), w_ref[...], preferred_element_type=F32)
    pr_ref[...] = p[:, :R_PROJ]
    pm_ref[...] = p[:, R_PROJ:]


def _in_proj(x2d, norm_w, w_cat_bf16, tm=512):
    n = x2d.shape[0]
    tm = math.gcd(tm, n)
    wtot = R_PROJ + M_PROJ_PAD
    return pl.pallas_call(
        _in_proj_kernel,
        grid=(n // tm,),
        in_specs=[
            pl.BlockSpec((tm, D_MODEL), lambda i: (i, 0)),
            pl.BlockSpec((1, D_MODEL), lambda i: (0, 0)),
            pl.BlockSpec((D_MODEL, wtot), lambda i: (0, 0)),
        ],
        out_specs=[
            pl.BlockSpec((tm, R_PROJ), lambda i: (i, 0)),
            pl.BlockSpec((tm, M_PROJ_PAD), lambda i: (i, 0)),
        ],
        out_shape=[
            jax.ShapeDtypeStruct((n, R_PROJ), F32),
            jax.ShapeDtypeStruct((n, M_PROJ_PAD), F32),
        ],
        compiler_params=pltpu.CompilerParams(
            dimension_semantics=("arbitrary",), vmem_limit_bytes=48 * 1024 * 1024),
        name="in_proj",
    )(x2d, norm_w, w_cat_bf16)


def _split3(x):
    hi = x.astype(BF16)
    r1 = x - hi.astype(F32)
    mid = r1.astype(BF16)
    lo = (r1 - mid.astype(F32)).astype(BF16)
    return hi, mid, lo


def _dot_f32_lhs(x, b_exact):
    return sum(jnp.dot(piece, b_exact, preferred_element_type=F32) for piece in _split3(x))


def _dot_f32_rhs(a_exact, x):
    return sum(jnp.dot(a_exact, piece, preferred_element_type=F32) for piece in _split3(x))


def _bdot(a, b):
    return jnp.dot(a.astype(BF16), b.astype(BF16), preferred_element_type=F32)


def _bdot_nt(a, b):
    return lax.dot_general(a.astype(BF16), b.astype(BF16), (((1,), (1,)), ((), ())), preferred_element_type=F32)


def _bdot_tn(a, b):
    return lax.dot_general(a.astype(BF16), b.astype(BF16), (((0,), (0,)), ((), ())), preferred_element_type=F32)


RWKV_ROWS_PER_STEP = 2


def _rwkv_kernel(p_ref, shift0_ref, wkv0_ref, mu_ref, w0_ref, w2_ref, a0_ref, a2_ref, g2_ref, kk_ref, ka_ref,
                 rk_ref, lnw_ref, lnb_ref, o_ref, wkv_out_ref, s_scr, prev_scr):
    c = pl.program_id(1)
    nc = pl.num_programs(1)
    L = CHUNK
    npairs = R_WIDTH // LANES
    nrows = p_ref.shape[0]

    @pl.when(c == 0)
    def _():
        for b in range(nrows):
            prev_scr[b] = jnp.broadcast_to(shift0_ref[b], prev_scr.shape[1:])
        s_scr[...] = wkv0_ref[...]

    li = lax.broadcasted_iota(jnp.int32, (LANES, LANES), 0)
    lj = lax.broadcasted_iota(jnp.int32, (LANES, LANES), 1)
    bd_mask = (li // R_HEAD) == (lj // R_HEAD)
    seg = bd_mask.astype(BF16)

    def head_sums(x):
        return jnp.concatenate([_dot_f32_lhs(x[:, LANES * j:LANES * (j + 1)], seg) for j in range(npairs)], axis=-1)

    ti = lax.broadcasted_iota(jnp.int32, (L, L), 0)
    si = lax.broadcasted_iota(jnp.int32, (L, L), 1)
    tri = (si <= ti).astype(BF16)
    row = lax.broadcasted_iota(jnp.int32, (L, 1), 0)

    rows = []
    for b in range(nrows):
        p = p_ref[b]
        p_prev = jnp.where(row == 0, prev_scr[b, 0:1, :], pltpu.roll(p, 1, 0))
        prev_scr[b] = jnp.broadcast_to(p[L - 1:L, :], prev_scr.shape[1:])
        pm = p + (p_prev - p) * mu_ref[...]
        r = pm[:, 0:R_WIDTH]
        k = pm[:, R_WIDTH:2 * R_WIDTH]
        v = pm[:, 2 * R_WIDTH:3 * R_WIDTH]
        wa = pm[:, 3 * R_WIDTH:3 * R_WIDTH + LANES]
        gd = pm[:, 3 * R_WIDTH + LANES:3 * R_WIDTH + 2 * LANES]
        w_log = -_softplus(-(w0_ref[...] + _dot(jnp.tanh(wa), w2_ref[...]))) - 0.5
        lw = -jnp.exp(w_log)
        a = _sigmoid(a0_ref[...] + _dot(wa, a2_ref[...]))
        g = _dot(_sigmoid(gd), g2_ref[...])
        kk = k * kk_ref[...]
        kk = kk / jnp.maximum(jnp.sqrt(head_sums(kk * kk)), 1e-12)
        k_mod = k * (1.0 + (a - 1.0) * ka_ref[...])
        cl = _dot_f32_rhs(tri, lw)
        e_in = jnp.exp(cl)
        e_neg = jnp.exp(-cl)
        rows.append(dict(At=-kk * jnp.exp(cl - lw), Bt=kk * a * e_neg, Kt=k_mod * e_neg, Rt=r * e_in, v=v,
                         p_last=e_in[L - 1:L, :], bonus=head_sums(r * k_mod * rk_ref[...]) * v, g=g))

    lane = lax.broadcasted_iota(jnp.int32, (L, LANES), 1)
    trow = lax.broadcasted_iota(jnp.int32, (L, LANES), 0)
    first = lane < R_HEAD
    strict = (lane % R_HEAD) < trow
    incl = (lane % R_HEAD) <= trow
    zeros = jnp.zeros((L, LANES), F32)

    units = [(b, j) for b in range(nrows) for j in range(npairs)]
    tile = lambda b, j, name: rows[b][name][:, LANES * j:LANES * (j + 1)]
    S = [s_scr[b, j] for b, j in units]
    bk = [jnp.concatenate([tile(b, j, "Bt"), tile(b, j, "Kt")], axis=0).astype(BF16) for b, j in units]
    G = []
    for u, (b, j) in enumerate(units):
        A_, R_ = tile(b, j, "At"), tile(b, j, "Rt")
        x4 = jnp.concatenate([jnp.where(first, A_, 0.0), jnp.where(first, 0.0, A_),
                              jnp.where(first, R_, 0.0), jnp.where(first, 0.0, R_)], axis=0)
        G.append(_bdot_nt(x4, bk[u]))
    ss = [_bdot_nt(jnp.concatenate([tile(b, j, "At"), tile(b, j, "Rt")], axis=0), S[u])
          for u, (b, j) in enumerate(units)]
    m0 = [jnp.where(strict, g_[0:L], 0.0) for g_ in G]
    m1 = [jnp.where(strict, g_[L:2 * L], 0.0) for g_ in G]
    x = []
    for u, (b, j) in enumerate(units):
        zv = jnp.concatenate([zeros, tile(b, j, "v")], axis=0).astype(BF16)
        x.append(ss[u][0:L] + jnp.where(first, _bdot(m0[u], zv), _bdot(m1[u], zv)))
    n0 = [m[:, 0:R_HEAD].astype(BF16) for m in m0]
    n1 = [m[:, 0:R_HEAD].astype(BF16) for m in m1]
    for i in range(6):
        for u in range(len(units)):
            xb = x[u].astype(BF16)
            x[u] = x[u] + jnp.where(first, _bdot(n0[u], xb), _bdot(n1[u], xb))
        if i < 5:
            n0 = [_bdot(n, n).astype(BF16) for n in n0]
            n1 = [_bdot(n, n).astype(BF16) for n in n1]
    outs = [[None] * npairs for _ in range(nrows)]
    for u, (b, j) in enumerate(units):
        uv = jnp.concatenate([x[u], tile(b, j, "v")], axis=0).astype(BF16)
        mo0 = jnp.where(incl, G[u][2 * L:3 * L], 0.0)
        mo1 = jnp.where(incl, G[u][3 * L:4 * L], 0.0)
        outs[b][j] = ss[u][L:2 * L] + jnp.where(first, _bdot(mo0, uv), _bdot(mo1, uv))
        S[u] = (S[u] + jnp.where(bd_mask, _bdot_tn(uv, bk[u]), 0.0)) * tile(b, j, "p_last")
    for u, (b, j) in enumerate(units):
        s_scr[b, j] = S[u]
    for b in range(nrows):
        o = jnp.concatenate(outs[b], axis=-1)
        mean = head_sums(o) * (1.0 / R_HEAD)
        d = o - mean
        var = head_sums(d * d) * (1.0 / R_HEAD)
        o = d * lax.rsqrt(var + R_GN_EPS)
        o_ref[b] = (o * lnw_ref[...] + lnb_ref[...] + rows[b]["bonus"]) * rows[b]["g"]

    @pl.when(c == nc - 1)
    def _():
        wkv_out_ref[...] = s_scr[...]


def _rwkv(proj_r, shift0, wkv0, wts):
    bsz, t, _ = proj_r.shape
    nc = t // CHUNK
    npairs = R_WIDTH // LANES
    w5 = wkv0.reshape(bsz, npairs, 2, R_HEAD, R_HEAD)
    zero = jnp.zeros_like(w5[:, :, 0])
    wkv_bd = jnp.concatenate([jnp.concatenate([w5[:, :, 0], zero], axis=-1),
                              jnp.concatenate([zero, w5[:, :, 1]], axis=-1)], axis=-2)
    full = lambda shape: pl.BlockSpec(shape, lambda b, c: (0,) * len(shape))
    nb = math.gcd(RWKV_ROWS_PER_STEP, bsz)
    o_r, s_bd = pl.pallas_call(
        _rwkv_kernel,
        grid=(bsz // nb, nc),
        in_specs=[
            pl.BlockSpec((nb, CHUNK, R_PROJ), lambda b, c: (b, c, 0)),
            pl.BlockSpec((nb, 1, R_PROJ), lambda b, c: (b, 0, 0)),
            pl.BlockSpec((nb, npairs, LANES, LANES), lambda b, c: (b, 0, 0, 0)),
            full((1, R_PROJ)), full((1, R_WIDTH)), full((LANES, R_WIDTH)), full((1, R_WIDTH)),
            full((LANES, R_WIDTH)), full((GATE_LORA, R_WIDTH)), full((1, R_WIDTH)), full((1, R_WIDTH)),
            full((1, R_WIDTH)), full((1, R_WIDTH)), full((1, R_WIDTH)),
        ],
        out_specs=[
            pl.BlockSpec((nb, CHUNK, R_WIDTH), lambda b, c: (b, c, 0)),
            pl.BlockSpec((nb, npairs, LANES, LANES), lambda b, c: (b, 0, 0, 0)),
        ],
        out_shape=[
            jax.ShapeDtypeStruct((bsz, t, R_WIDTH), F32),
            jax.ShapeDtypeStruct((bsz, npairs, LANES, LANES), F32),
        ],
        scratch_shapes=[pltpu.VMEM((nb, npairs, LANES, LANES), F32), pltpu.VMEM((nb, SUBLANES, R_PROJ), F32)],
        compiler_params=pltpu.CompilerParams(
            dimension_semantics=("arbitrary", "arbitrary"), vmem_limit_bytes=40 * 1024 * 1024),
        name="rwkv",
    )(proj_r, shift0.reshape(bsz, 1, R_PROJ), wkv_bd, *wts)
    new_wkv = jnp.stack([s_bd[:, :, :R_HEAD, :R_HEAD], s_bd[:, :, R_HEAD:, R_HEAD:]], axis=2)
    return o_r, new_wkv.reshape(bsz, R_HEADS, R_HEAD, R_HEAD)


def _ssd_kernel(p_ref, conv0_ref, ssm0_ref, cw_ref, cb_ref, dtb_ref, alog_ref, dvec_ref, nw_ref,
                o_ref, conv_out_ref, ssm_out_ref, h_scr, prev_scr):
    c = pl.program_id(1)
    nc = pl.num_programs(1)
    L = CHUNK

    @pl.when(c == 0)
    def _():
        prev_scr[...] = conv0_ref[0]
        h_scr[...] = ssm0_ref[0]

    p = p_ref[0]
    z = p[:, 0:M_WIDTH]
    xbc = p[:, M_WIDTH:M_WIDTH + CONV_DIM]
    dt_raw = p[:, M_WIDTH + CONV_DIM:]
    prev8 = prev_scr[...]
    row8 = lax.broadcasted_iota(jnp.int32, (SUBLANES, 1), 0)
    conv = xbc * cw_ref[CONV_W - 1:CONV_W, :]
    for j in range(1, CONV_W):
        xs_j = pltpu.roll(xbc, j, 0)
        first = jnp.where(row8 >= j, xs_j[0:SUBLANES], pltpu.roll(prev8, j, 0))
        shifted = jnp.concatenate([first, xs_j[SUBLANES:]], axis=0)
        conv = conv + shifted * cw_ref[CONV_W - 1 - j:CONV_W - j, :]
    prev_scr[...] = xbc[L - SUBLANES:L, :]
    act = _silu(conv + cb_ref[...])
    xs = act[:, 0:M_WIDTH]
    Bm = act[:, M_WIDTH:M_WIDTH + M_GROUPS * M_STATE]
    Cm = act[:, M_WIDTH + M_GROUPS * M_STATE:]
    dt = _softplus(dt_raw + dtb_ref[...])
    A = -jnp.exp(alog_ref[...])
    ti = lax.broadcasted_iota(jnp.int32, (L, L), 0)
    si = lax.broadcasted_iota(jnp.int32, (L, L), 1)
    causal = si <= ti
    cs = _dot(causal.astype(F32), dt * A)
    cs_t = cs.T
    ys = []
    for h in range(M_HEADS):
        g = h // (M_HEADS // M_GROUPS)
        sl = slice(h * M_HEADDIM, (h + 1) * M_HEADDIM)
        gs = slice(g * M_STATE, (g + 1) * M_STATE)
        xh = xs[:, sl]
        Bg = Bm[:, gs]
        Cg = Cm[:, gs]
        cs_col = cs[:, h:h + 1]
        cs_row = cs_t[h:h + 1, :]
        decay = jnp.where(causal, jnp.exp(jnp.where(causal, cs_col - cs_row, 0.0)), 0.0)
        xdt = xh * dt[:, h:h + 1]
        cb = _dot_nt(Cg, Bg)
        hin = h_scr[h]
        y = _dot(cb * decay, xdt) + jnp.exp(cs_col) * _dot_nt(Cg, hin)
        cs_last = cs[L - 1:L, h:h + 1]
        to_end = jnp.exp(cs_last - cs_col)
        h_scr[h] = hin * jnp.exp(cs_last) + _dot_tn(xdt * to_end, Bg)
        ys.append(y + dvec_ref[:, h:h + 1] * xh)
    y = jnp.concatenate(ys, axis=-1) * _silu(z)
    gw = M_WIDTH // M_GROUPS
    parts = []
    for g in range(M_GROUPS):
        yg = y[:, g * gw:(g + 1) * gw]
        parts.append(yg * lax.rsqrt(jnp.mean(yg * yg, axis=-1, keepdims=True) + NORM_EPS))
    o_ref[0] = jnp.concatenate(parts, axis=-1) * nw_ref[...]

    @pl.when(c == nc - 1)
    def _():
        conv_out_ref[0] = prev_scr[...]
        ssm_out_ref[0] = h_scr[...]


def _ssd(proj_m, conv0_8, ssm0, wts):
    bsz, t, _ = proj_m.shape
    nc = t // CHUNK
    full = lambda shape: pl.BlockSpec(shape, lambda b, c: (0,) * len(shape))
    return pl.pallas_call(
        _ssd_kernel,
        grid=(bsz, nc),
        in_specs=[
            pl.BlockSpec((1, CHUNK, M_PROJ_PAD), lambda b, c: (b, c, 0)),
            pl.BlockSpec((1, SUBLANES, CONV_DIM), lambda b, c: (b, 0, 0)),
            pl.BlockSpec((1, M_HEADS, M_HEADDIM, M_STATE), lambda b, c: (b, 0, 0, 0)),
            full((CONV_W, CONV_DIM)), full((1, CONV_DIM)), full((1, LANES)), full((1, LANES)),
            full((1, LANES)), full((1, M_WIDTH)),
        ],
        out_specs=[
            pl.BlockSpec((1, CHUNK, M_WIDTH), lambda b, c: (b, c, 0)),
            pl.BlockSpec((1, SUBLANES, CONV_DIM), lambda b, c: (b, 0, 0)),
            pl.BlockSpec((1, M_HEADS, M_HEADDIM, M_STATE), lambda b, c: (b, 0, 0, 0)),
        ],
        out_shape=[
            jax.ShapeDtypeStruct((bsz, t, M_WIDTH), F32),
            jax.ShapeDtypeStruct((bsz, SUBLANES, CONV_DIM), F32),
            jax.ShapeDtypeStruct((bsz, M_HEADS, M_HEADDIM, M_STATE), F32),
        ],
        scratch_shapes=[pltpu.VMEM((M_HEADS, M_HEADDIM, M_STATE), F32), pltpu.VMEM((SUBLANES, CONV_DIM), F32)],
        compiler_params=pltpu.CompilerParams(
            dimension_semantics=("arbitrary", "arbitrary"), vmem_limit_bytes=40 * 1024 * 1024),
        name="ssd",
    )(proj_m, conv0_8, ssm0, *wts)


def _top16_rows(s, rowid, big):
    vals, ids = [], []
    for _ in range(P_TOPK):
        m = jnp.max(s, axis=0, keepdims=True)
        idx = jnp.min(jnp.where(s == m, rowid, big), axis=0, keepdims=True)
        s = jnp.where(rowid == idx, -jnp.inf, s)
        vals.append(m)
        ids.append(idx)
    return vals, ids


def _route_kernel(or_ref, om_ref, x_ref, wr_ref, wm_ref, n2_ref, wq_ref, keys_ref,
                  h_ref, xn_ref, off_ref, gate_ref, sv_scr, si_scr):
    tm = x_ref.shape[0]
    h = (x_ref[...]
         + jnp.dot(or_ref[...].astype(BF16), wr_ref[...], preferred_element_type=F32)
         + jnp.dot(om_ref[...].astype(BF16), wm_ref[...], preferred_element_type=F32))
    h_ref[...] = h
    xn = h * lax.rsqrt(jnp.mean(h * h, axis=-1, keepdims=True) + NORM_EPS) * n2_ref[...]
    xn_ref[...] = xn
    xn_b = xn.astype(BF16)
    rowid = lax.broadcasted_iota(jnp.int32, (N_KEYS, tm), 0)

    def sub_body(hd, carry):
        res = []
        for half in range(2):
            q = jnp.dot(xn_b, wq_ref[2 * hd + half], preferred_element_type=F32)
            s_t = _dot_nt(keys_ref[half], q)
            res.append(_top16_rows(s_t, rowid, N_KEYS))
        for half, (vals, ids) in enumerate(res):
            sv_scr[2 * hd + half] = jnp.concatenate(vals, axis=0)
            si_scr[2 * hd + half] = jnp.concatenate(ids, axis=0)
        return carry

    lax.fori_loop(0, P_HEADS, sub_body, 0)

    r16 = lax.broadcasted_iota(jnp.int32, (16, 1), 0)
    r8 = lax.broadcasted_iota(jnp.int32, (8, 1), 0)

    def head_body(hd, carry):
        sv0, sv1 = sv_scr[2 * hd], sv_scr[2 * hd + 1]
        si0, si1 = si_scr[2 * hd], si_scr[2 * hd + 1]
        cand, cid, flat = [], [], []
        for i in range(8):
            n_i = P_TOPK // (i + 1)
            rows = 16 if i == 0 else 8
            rr = r16 if i == 0 else r8
            cand.append(jnp.where(rr < n_i, sv0[i:i + 1] + sv1[0:rows], -jnp.inf))
            cid.append(si0[i:i + 1] * N_KEYS + si1[0:rows])
            flat.append(jnp.broadcast_to(i * P_TOPK + rr, (rows, tm)))
        cand.append(sv0[8:16] + sv1[0:1])
        cid.append(si0[8:16] * N_KEYS + si1[0:1])
        flat.append(jnp.broadcast_to((8 + r8) * P_TOPK, (8, tm)))
        cand = jnp.concatenate(cand, axis=0)
        cid = jnp.concatenate(cid, axis=0)
        flat = jnp.concatenate(flat, axis=0)
        tops, eids = [], []
        for _ in range(P_TOPK):
            m = jnp.max(cand, axis=0, keepdims=True)
            sel = jnp.min(jnp.where(cand == m, flat, 4 * P_TOPK * P_TOPK), axis=0, keepdims=True)
            hit = flat == sel
            eids.append(jnp.sum(jnp.where(hit, cid, 0), axis=0, keepdims=True))
            cand = jnp.where(hit, -jnp.inf, cand)
            tops.append(m)
        top_s = jnp.concatenate(tops, axis=0)
        eid = jnp.concatenate(eids, axis=0)
        ex = jnp.exp(top_s - top_s[0:1])
        gate = ex / jnp.sum(ex, axis=0, keepdims=True)
        prow = lax.broadcasted_iota(jnp.int32, (P_TOPK, 1), 0)
        upper = ((prow % 8) >= 4).astype(jnp.int32)
        si_scr[hd] = TAB_PAD + TAB_ROWS_PER_EXPERT * eid - TAB_ROWS_PER_EXPERT * upper
        sv_scr[hd] = gate
        return carry

    lax.fori_loop(0, P_HEADS, head_body, 0)
    off_ref[...] = jnp.concatenate([si_scr[hd] for hd in range(P_HEADS)], axis=0).T
    gate_ref[...] = jnp.concatenate([sv_scr[hd] for hd in range(P_HEADS)], axis=0).T


def _route(o_r, o_m, x2d, w_out_r, w_out_m, norm2_w, wq3, keys, tm=512):
    n = x2d.shape[0]
    tm = math.gcd(tm, n)
    full = lambda shape: pl.BlockSpec(shape, lambda i: (0,) * len(shape))
    return pl.pallas_call(
        _route_kernel,
        grid=(n // tm,),
        in_specs=[
            pl.BlockSpec((tm, R_WIDTH), lambda i: (i, 0)),
            pl.BlockSpec((tm, M_WIDTH), lambda i: (i, 0)),
            pl.BlockSpec((tm, D_MODEL), lambda i: (i, 0)),
            full((R_WIDTH, D_MODEL)), full((M_WIDTH, D_MODEL)), full((1, D_MODEL)),
            full((2 * P_HEADS, D_MODEL, P_QDIM // 2)), full((2, N_KEYS, P_QDIM // 2)),
        ],
        out_specs=[
            pl.BlockSpec((tm, D_MODEL), lambda i: (i, 0)),
            pl.BlockSpec((tm, D_MODEL), lambda i: (i, 0)),
            pl.BlockSpec((tm, PAIRS), lambda i: (i, 0)),
            pl.BlockSpec((tm, PAIRS), lambda i: (i, 0)),
        ],
        out_shape=[
            jax.ShapeDtypeStruct((n, D_MODEL), F32),
            jax.ShapeDtypeStruct((n, D_MODEL), F32),
            jax.ShapeDtypeStruct((n, PAIRS), jnp.int32),
            jax.ShapeDtypeStruct((n, PAIRS), F32),
        ],
        scratch_shapes=[pltpu.VMEM((2 * P_HEADS, P_TOPK, tm), F32), pltpu.VMEM((2 * P_HEADS, P_TOPK, tm), jnp.int32)],
        compiler_params=pltpu.CompilerParams(
            dimension_semantics=("arbitrary",), vmem_limit_bytes=48 * 1024 * 1024),
        name="route",
    )(o_r, o_m, x2d, w_out_r, w_out_m, norm2_w, wq3, keys)


def _pack_table(tab):
    tb = tab.astype(BF16)
    half = D_MODEL // 2
    lo = lax.bitcast_convert_type(tb[:, :half], jnp.uint16).astype(jnp.uint32)
    hi = lax.bitcast_convert_type(tb[:, half:], jnp.uint16).astype(jnp.uint32)
    w = (lo | (hi << 16)).reshape(N_EXPERTS * TAB_ROWS_PER_EXPERT, LANES)
    return jnp.pad(w, ((TAB_PAD, TAB_PAD), (0, 0)))


def _merged_tile(tab_vmem, off_ref, t, pa, pb, low):
    wa = tab_vmem[pl.ds(off_ref[t, pa], SUBLANES), :]
    wb = tab_vmem[pl.ds(off_ref[t, pb], SUBLANES), :]
    return jnp.where(low, wa, wb)


def _unpack_words(w):
    lo = pltpu.bitcast(w << 16, F32)
    hi = pltpu.bitcast(w & jnp.uint32(0xFFFF0000), F32)
    return lo, hi


def _peer_u_kernel(off_ref, x3_ref, gate_ref, tab_vmem, c_ref, r_scr):
    tb = x3_ref.shape[0]
    row = lax.broadcasted_iota(jnp.int32, (SUBLANES, LANES), 0)
    low = row < 4
    m2 = (row % 4) < 2
    m1 = (row % 2) == 0
    ones = jnp.ones((SUBLANES, LANES), F32)

    def fold(A, B, sh, m):
        return jnp.where(m, A + pltpu.roll(A, SUBLANES - sh, 0), B + pltpu.roll(B, sh, 0))

    def group_body(tg, carry):
        for j in range(SUBLANES):
            t = tg * SUBLANES + j
            x = x3_ref[t]
            xr = pltpu.roll(x, 4, 0)
            xa = jnp.where(low, x, xr)
            xb = jnp.where(low, xr, x)

            def prod(pa, pb):
                lo, hi = _unpack_words(_merged_tile(tab_vmem, off_ref, t, pa, pb, low))
                return lo * xa + hi * xb

            for g in range(PAIRS // 8):
                b = g * 8
                n1 = fold(prod(b + 0, b + 4), prod(b + 2, b + 6), 2, m2)
                n2 = fold(prod(b + 1, b + 5), prod(b + 3, b + 7), 2, m2)
                r_scr[j, pl.ds(b, 8), :] = fold(n1, n2, 1, m1)
        rows = [_dot_nt(ones, r_scr[j])[0:1, :] for j in range(SUBLANES)]
        hid = jnp.concatenate(rows, axis=0)
        sl = pl.ds(pl.multiple_of(tg * SUBLANES, SUBLANES), SUBLANES)
        gelu = 0.5 * hid * (1.0 + lax.erf(hid * (1.0 / math.sqrt(2.0))))
        c_ref[sl, :] = gelu * gate_ref[sl, :]
        return carry

    lax.fori_loop(0, tb // SUBLANES, group_body, 0)


def _peer_u(off, x3, gate, tab, tb=128):
    n = off.shape[0]
    tb = math.gcd(tb, n)
    return pl.pallas_call(
        _peer_u_kernel,
        grid=(n // tb,),
        in_specs=[
            pl.BlockSpec((tb, PAIRS), lambda i: (i, 0), memory_space=pltpu.SMEM),
            pl.BlockSpec((tb, SUBLANES, LANES), lambda i: (i, 0, 0)),
            pl.BlockSpec((tb, PAIRS), lambda i: (i, 0)),
            pl.BlockSpec(memory_space=pltpu.VMEM),
        ],
        out_specs=pl.BlockSpec((tb, PAIRS), lambda i: (i, 0)),
        out_shape=jax.ShapeDtypeStruct((n, PAIRS), F32),
        scratch_shapes=[pltpu.VMEM((SUBLANES, PAIRS, LANES), F32)],
        compiler_params=pltpu.CompilerParams(
            dimension_semantics=("arbitrary",), vmem_limit_bytes=56 * 1024 * 1024),
        name="peer_u",
    )(off, x3, gate, tab)


V_TOKENS_PER_STEP = 4
V_ROWS = 2 * SUBLANES


def _peer_v_kernel(off_ref, c_ref, h3_ref, fw_ref, tab_vmem, y_ref, crep_scr):
    tb = h3_ref.shape[0]
    kw = PAIRS * V_ROWS
    fw = fw_ref[...]
    rp = lax.broadcasted_iota(jnp.int32, (PAIRS, kw), 0)
    rc = lax.broadcasted_iota(jnp.int32, (PAIRS, kw), 1)
    rep = ((rc // V_ROWS == rp) & (rc % V_ROWS < SUBLANES)).astype(BF16)
    crep_scr[...] = _dot_f32_lhs(c_ref[...], rep)
    m = lax.broadcasted_iota(jnp.int32, (SUBLANES, kw), 0)
    col = lax.broadcasted_iota(jnp.int32, (SUBLANES, kw), 1)
    sel = m == (col % 2) * 4 + (col % V_ROWS) // 2

    def group_body(tg, carry):
        toks = [tg * V_TOKENS_PER_STEP + j for j in range(V_TOKENS_PER_STEP)]
        offs = [off_ref.at[t] for t in toks]
        tiles = [[] for _ in toks]
        for p in range(PAIRS):
            back = TAB_ROWS_PER_EXPERT if (p % 8) >= 4 else 0
            for j in range(V_TOKENS_PER_STEP):
                tiles[j].append(pltpu.bitcast(tab_vmem[pl.ds(offs[j][p] + back, SUBLANES), :], BF16))
        for j, t in enumerate(toks):
            lhs = jnp.where(sel, crep_scr[pl.ds(t, 1), :], 0.0)
            hi = pltpu.bitcast(pltpu.bitcast(lhs, jnp.uint32) & jnp.uint32(0xFFFF0000), F32)
            x16 = jnp.concatenate([hi, lhs - hi], axis=0).astype(BF16)
            acc = jnp.dot(x16, jnp.concatenate(tiles[j], axis=0), preferred_element_type=F32)
            y_ref[t] = acc[0:SUBLANES] + acc[SUBLANES:]
        return carry

    lax.fori_loop(0, tb // V_TOKENS_PER_STEP, group_body, 0)
    hn = h3_ref[...] + y_ref[...]
    ms = jnp.sum(jnp.sum(hn * hn, axis=2, keepdims=True), axis=1, keepdims=True) * (1.0 / D_MODEL)
    y_ref[...] = hn * lax.rsqrt(ms + NORM_EPS) * fw


def _peer_v(off, c, h3, fw, tab, tb=128):
    n = off.shape[0]
    tb = math.gcd(tb, n)
    return pl.pallas_call(
        _peer_v_kernel,
        grid=(n // tb,),
        in_specs=[
            pl.BlockSpec((tb, PAIRS), lambda i: (i, 0), memory_space=pltpu.SMEM),
            pl.BlockSpec((tb, PAIRS), lambda i: (i, 0)),
            pl.BlockSpec((tb, SUBLANES, LANES), lambda i: (i, 0, 0)),
            pl.BlockSpec((SUBLANES, LANES), lambda i: (0, 0)),
            pl.BlockSpec(memory_space=pltpu.VMEM),
        ],
        out_specs=pl.BlockSpec((tb, SUBLANES, LANES), lambda i: (i, 0, 0)),
        out_shape=jax.ShapeDtypeStruct((n, SUBLANES, LANES), F32),
        scratch_shapes=[pltpu.VMEM((tb, PAIRS * V_ROWS), F32)],
        compiler_params=pltpu.CompilerParams(
            dimension_semantics=("arbitrary",), vmem_limit_bytes=56 * 1024 * 1024),
        name="peer_v",
    )(off, c, h3, fw, tab)


def _prep_weights(norm1_w, w_in, rwkv_mu, rwkv_w0, rwkv_w2, rwkv_a0, rwkv_a2, rwkv_g2, rwkv_k_k, rwkv_k_a,
                  rwkv_r_k, rwkv_ln_w, rwkv_ln_b, mamba_conv_w, mamba_conv_b, mamba_dt_bias, mamba_A_log,
                  mamba_D, mamba_norm_w, w_out, norm2_w, peer_w_q, peer_sub_keys, peer_u, peer_v, final_norm_w):
    row = lambda a: a.reshape(1, -1)
    pad_lanes = lambda a: jnp.pad(row(a), ((0, 0), (0, LANES - a.shape[-1])))
    w_cat = jnp.pad(w_in, ((0, 0), (0, M_PROJ_PAD - M_PROJ))).astype(BF16)
    zeros = jnp.zeros((DECAY_LORA, R_WIDTH), F32)
    rw = (row(rwkv_mu), row(rwkv_w0), jnp.concatenate([rwkv_w2, zeros], 0), row(rwkv_a0),
          jnp.concatenate([zeros, rwkv_a2], 0), rwkv_g2, row(rwkv_k_k), row(rwkv_k_a), row(rwkv_r_k),
          row(rwkv_ln_w), row(rwkv_ln_b))
    mw = (mamba_conv_w, row(mamba_conv_b), pad_lanes(mamba_dt_bias), pad_lanes(mamba_A_log),
          pad_lanes(mamba_D), row(mamba_norm_w))
    wq3 = peer_w_q.reshape(D_MODEL, 2 * P_HEADS, P_QDIM // 2).transpose(1, 0, 2).astype(BF16)
    route_w = (w_out[:R_WIDTH].astype(BF16), w_out[R_WIDTH:].astype(BF16), row(norm2_w), wq3, peer_sub_keys)
    return dict(norm1=row(norm1_w), w_cat=w_cat, rw=rw, mw=mw, route_w=route_w,
                tab_u=_pack_table(peer_u), tab_v=_pack_table(peer_v),
                fw=final_norm_w.reshape(SUBLANES, LANES))


def _trunk(x, shift0, wkv0, conv0, ssm0, w):
    bsz, t, _ = x.shape
    n = bsz * t
    x2d = x.reshape(n, D_MODEL)
    proj_r, proj_m = _in_proj(x2d, w["norm1"], w["w_cat"])
    proj_r = proj_r.reshape(bsz, t, R_PROJ)
    proj_m = proj_m.reshape(bsz, t, M_PROJ_PAD)
    o_r, new_wkv = _rwkv(proj_r, shift0, wkv0, w["rw"])
    conv0_8 = jnp.pad(conv0, ((0, 0), (SUBLANES - (CONV_W - 1), 0), (0, 0)))
    o_m, new_conv8, new_ssm = _ssd(proj_m, conv0_8, ssm0, w["mw"])
    h, xn, off, gate = _route(o_r.reshape(n, R_WIDTH), o_m.reshape(n, M_WIDTH), x2d, *w["route_w"])
    c = _peer_u(off, xn.reshape(n, SUBLANES, LANES), gate, w["tab_u"])
    y = _peer_v(off, c, h.reshape(n, SUBLANES, LANES), w["fw"], w["tab_v"])
    new_shift = proj_r[:, -1, :]
    new_conv = new_conv8[:, SUBLANES - (CONV_W - 1):, :]
    return y.reshape(bsz, t, D_MODEL), new_shift[None], new_wkv[None], new_conv[None], new_ssm[None]


def kernel(x_prompt, x_sample, state_rwkv_shift, state_rwkv_wkv, state_mamba_conv, state_mamba_ssm, norm1_w, w_in, rwkv_mu, rwkv_w0, rwkv_w2, rwkv_a0, rwkv_a2, rwkv_g2, rwkv_k_k, rwkv_k_a, rwkv_r_k, rwkv_ln_w, rwkv_ln_b, mamba_conv_w, mamba_conv_b, mamba_dt_bias, mamba_A_log, mamba_D, mamba_norm_w, w_out, norm2_w, peer_w_q, peer_sub_keys, peer_u, peer_v, final_norm_w):
    w = _prep_weights(norm1_w[0], w_in[0], rwkv_mu[0], rwkv_w0[0], rwkv_w2[0], rwkv_a0[0], rwkv_a2[0], rwkv_g2[0],
                      rwkv_k_k[0], rwkv_k_a[0], rwkv_r_k[0].reshape(-1), rwkv_ln_w[0], rwkv_ln_b[0],
                      mamba_conv_w[0], mamba_conv_b[0], mamba_dt_bias[0], mamba_A_log[0], mamba_D[0],
                      mamba_norm_w[0], w_out[0], norm2_w[0], peer_w_q[0], peer_sub_keys[0], peer_u[0], peer_v[0],
                      final_norm_w)
    bp = x_prompt.shape[0]
    zp = lambda *s: jnp.zeros((bp,) + s, F32)
    yp, *sp = _trunk(x_prompt, zp(R_PROJ), zp(R_HEADS, R_HEAD, R_HEAD), zp(CONV_W - 1, CONV_DIM),
                     zp(M_HEADS, M_HEADDIM, M_STATE), w)
    ys, *ss = _trunk(x_sample, state_rwkv_shift[0], state_rwkv_wkv[0], state_mamba_conv[0], state_mamba_ssm[0], w)
    return (yp, ys, *sp, *ss)
```

```python
import functools
import math

import jax
import jax.numpy as jnp
import numpy as np
from jax import lax
from jax.experimental import pallas as pl
from jax.experimental.pallas import tpu as pltpu

LANES = 128
SUBLANES = 8
VMEM_BYTES_V7X = 64 * 1024 * 1024

D_MODEL = 1024
CHUNK = 64
R_WIDTH = 512
R_HEAD = 64
R_HEADS = R_WIDTH // R_HEAD
DECAY_LORA = 64
AAA_LORA = 64
GATE_LORA = 128
R_PROJ = 3 * R_WIDTH + DECAY_LORA + AAA_LORA + GATE_LORA
R_GN_EPS = R_HEAD * 1e-5
M_WIDTH = 512
M_HEADDIM = 64
M_HEADS = M_WIDTH // M_HEADDIM
M_GROUPS = 2
M_STATE = 64
CONV_W = 4
CONV_DIM = M_WIDTH + 2 * M_GROUPS * M_STATE
M_PROJ = M_WIDTH + CONV_DIM + M_HEADS
M_PROJ_PAD = M_WIDTH + CONV_DIM + LANES
N_KEYS = 128
N_EXPERTS = N_KEYS * N_KEYS
P_HEADS = 8
P_QDIM = 256
P_TOPK = 16
PAIRS = P_HEADS * P_TOPK
NORM_EPS = 1e-6

HI = lax.Precision.HIGHEST
F32 = jnp.float32
BF16 = jnp.bfloat16

TAB_ROWS_PER_EXPERT = 4
TAB_PAD = 8
TAB_ROWS = N_EXPERTS * TAB_ROWS_PER_EXPERT + 2 * TAB_PAD


def _dot(a, b, precision=HI):
    return jnp.dot(a, b, precision=precision, preferred_element_type=F32)


def _dot_nt(a, b, precision=HI):
    return lax.dot_general(a, b, (((1,), (1,)), ((), ())), precision=precision, preferred_element_type=F32)


def _dot_tn(a, b, precision=HI):
    return lax.dot_general(a, b, (((0,), (0,)), ((), ())), precision=precision, preferred_element_type=F32)


def _softplus(x):
    return jnp.maximum(x, 0.0) + jnp.log(1.0 + jnp.exp(-jnp.abs(x)))


def _sigmoid(x):
    return 1.0 / (1.0 + jnp.exp(-x))


def _silu(x):
    return x * _sigmoid(x)


def _in_proj_kernel(x_ref, nw_ref, w_ref, pr_ref, pm_ref):
    x = x_ref[...]
    xn = x * lax.rsqrt(jnp.mean(x * x, axis=-1, keepdims=True) + NORM_EPS) * nw_ref[...]
    p = jnp.dot(xn.astype(BF16), w_ref[...], preferred_element_type=F32)
    pr_ref[...] = p[:, :R_PROJ]
    pm_ref[...] = p[:, R_PROJ:]


def _in_proj(x2d, norm_w, w_cat_bf16, tm=512):
    n = x2d.shape[0]
    tm = math.gcd(tm, n)
    wtot = R_PROJ + M_PROJ_PAD
    return pl.pallas_call(
        _in_proj_kernel,
        grid=(n // tm,),
        in_specs=[
            pl.BlockSpec((tm, D_MODEL), lambda i: (i, 0)),
            pl.BlockSpec((1, D_MODEL), lambda i: (0, 0)),
            pl.BlockSpec((D_MODEL, wtot), lambda i: (0, 0)),
        ],
        out_specs=[
            pl.BlockSpec((tm, R_PROJ), lambda i: (i, 0)),
            pl.BlockSpec((tm, M_PROJ_PAD), lambda i: (i, 0)),
        ],
        out_shape=[
            jax.ShapeDtypeStruct((n, R_PROJ), F32),
            jax.ShapeDtypeStruct((n, M_PROJ_PAD), F32),
        ],
        compiler_params=pltpu.CompilerParams(
            dimension_semantics=("arbitrary",), vmem_limit_bytes=48 * 1024 * 1024),
        name="in_proj",
    )(x2d, norm_w, w_cat_bf16)


def _split3(x):
    hi = x.astype(BF16)
    r1 = x - hi.astype(F32)
    mid = r1.astype(BF16)
    lo = (r1 - mid.astype(F32)).astype(BF16)
    return hi, mid, lo


def _dot_f32_lhs(x, b_exact):
    return sum(jnp.dot(piece, b_exact, preferred_element_type=F32) for piece in _split3(x))


def _dot_f32_rhs(a_exact, x):
    return sum(jnp.dot(a_exact, piece, preferred_element_type=F32) for piece in _split3(x))


def _bdot(a, b):
    return jnp.dot(a.astype(BF16), b.astype(BF16), preferred_element_type=F32)


def _bdot_nt(a, b):
    return lax.dot_general(a.astype(BF16), b.astype(BF16), (((1,), (1,)), ((), ())), preferred_element_type=F32)


def _bdot_tn(a, b):
    return lax.dot_general(a.astype(BF16), b.astype(BF16), (((0,), (0,)), ((), ())), preferred_element_type=F32)


RWKV_ROWS_PER_STEP = 2


def _rwkv_kernel(p_ref, shift0_ref, wkv0_ref, mu_ref, w0_ref, w2_ref, a0_ref, a2_ref, g2_ref, kk_ref, ka_ref,
                 rk_ref, lnw_ref, lnb_ref, o_ref, wkv_out_ref, s_scr, prev_scr):
    c = pl.program_id(1)
    nc = pl.num_programs(1)
    L = CHUNK
    npairs = R_WIDTH // LANES
    nrows = p_ref.shape[0]

    @pl.when(c == 0)
    def _():
        for b in range(nrows):
            prev_scr[b] = jnp.broadcast_to(shift0_ref[b], prev_scr.shape[1:])
        s_scr[...] = wkv0_ref[...]

    li = lax.broadcasted_iota(jnp.int32, (LANES, LANES), 0)
    lj = lax.broadcasted_iota(jnp.int32, (LANES, LANES), 1)
    bd_mask = (li // R_HEAD) == (lj // R_HEAD)
    seg = bd_mask.astype(BF16)

    def head_sums(x):
        return jnp.concatenate([_dot_f32_lhs(x[:, LANES * j:LANES * (j + 1)], seg) for j in range(npairs)], axis=-1)

    ti = lax.broadcasted_iota(jnp.int32, (L, L), 0)
    si = lax.broadcasted_iota(jnp.int32, (L, L), 1)
    tri = (si <= ti).astype(BF16)
    row = lax.broadcasted_iota(jnp.int32, (L, 1), 0)

    rows = []
    for b in range(nrows):
        p = p_ref[b]
        p_prev = jnp.where(row == 0, prev_scr[b, 0:1, :], pltpu.roll(p, 1, 0))
        prev_scr[b] = jnp.broadcast_to(p[L - 1:L, :], prev_scr.shape[1:])
        pm = p + (p_prev - p) * mu_ref[...]
        r = pm[:, 0:R_WIDTH]
        k = pm[:, R_WIDTH:2 * R_WIDTH]
        v = pm[:, 2 * R_WIDTH:3 * R_WIDTH]
        wa = pm[:, 3 * R_WIDTH:3 * R_WIDTH + LANES]
        gd = pm[:, 3 * R_WIDTH + LANES:3 * R_WIDTH + 2 * LANES]
        w_log = -_softplus(-(w0_ref[...] + _dot(jnp.tanh(wa), w2_ref[...]))) - 0.5
        lw = -jnp.exp(w_log)
        a = _sigmoid(a0_ref[...] + _dot(wa, a2_ref[...]))
        g = _dot(_sigmoid(gd), g2_ref[...])
        kk = k * kk_ref[...]
        kk = kk / jnp.maximum(jnp.sqrt(head_sums(kk * kk)), 1e-12)
        k_mod = k * (1.0 + (a - 1.0) * ka_ref[...])
        cl = _dot_f32_rhs(tri, lw)
        e_in = jnp.exp(cl)
        e_neg = jnp.exp(-cl)
        rows.append(dict(At=-kk * jnp.exp(cl - lw), Bt=kk * a * e_neg, Kt=k_mod * e_neg, Rt=r * e_in, v=v,
                         p_last=e_in[L - 1:L, :], bonus=head_sums(r * k_mod * rk_ref[...]) * v, g=g))

    lane = lax.broadcasted_iota(jnp.int32, (L, LANES), 1)
    trow = lax.broadcasted_iota(jnp.int32, (L, LANES), 0)
    first = lane < R_HEAD
    strict = (lane % R_HEAD) < trow
    incl = (lane % R_HEAD) <= trow
    zeros = jnp.zeros((L, LANES), F32)

    units = [(b, j) for b in range(nrows) for j in range(npairs)]
    tile = lambda b, j, name: rows[b][name][:, LANES * j:LANES * (j + 1)]
    S = [s_scr[b, j] for b, j in units]
    bk = [jnp.concatenate([tile(b, j, "Bt"), tile(b, j, "Kt")], axis=0).astype(BF16) for b, j in units]
    G = []
    for u, (b, j) in enumerate(units):
        A_, R_ = tile(b, j, "At"), tile(b, j, "Rt")
        x4 = jnp.concatenate([jnp.where(first, A_, 0.0), jnp.where(first, 0.0, A_),
                              jnp.where(first, R_, 0.0), jnp.where(first, 0.0, R_)], axis=0)
        G.append(_bdot_nt(x4, bk[u]))
    ss = [_bdot_nt(jnp.concatenate([tile(b, j, "At"), tile(b, j, "Rt")], axis=0), S[u])
          for u, (b, j) in enumerate(units)]
    m0 = [jnp.where(strict, g_[0:L], 0.0) for g_ in G]
    m1 = [jnp.where(strict, g_[L:2 * L], 0.0) for g_ in G]
    x = []
    for u, (b, j) in enumerate(units):
        zv = jnp.concatenate([zeros, tile(b, j, "v")], axis=0).astype(BF16)
        x.append(ss[u][0:L] + jnp.where(first, _bdot(m0[u], zv), _bdot(m1[u], zv)))
    n0 = [m[:, 0:R_HEAD].astype(BF16) for m in m0]
    n1 = [m[:, 0:R_HEAD].astype(BF16) for m in m1]
    for i in range(6):
        for u in range(len(units)):
            xb = x[u].astype(BF16)
            x[u] = x[u] + jnp.where(first, _bdot(n0[u], xb), _bdot(n1[u], xb))
        if i < 5:
            n0 = [_bdot(n, n).astype(BF16) for n in n0]
            n1 = [_bdot(n, n).astype(BF16) for n in n1]
    outs = [[None] * npairs for _ in range(nrows)]
    for u, (b, j) in enumerate(units):
        uv = jnp.concatenate([x[u], tile(b, j, "v")], axis=0).astype(BF16)
        mo0 = jnp.where(incl, G[u][2 * L:3 * L], 0.0)
        mo1 = jnp.where(incl, G[u][3 * L:4 * L], 0.0)
        outs[b][j] = ss[u][L:2 * L] + jnp.where(first, _bdot(mo0, uv), _bdot(mo1, uv))
        S[u] = (S[u] + jnp.where(bd_mask, _bdot_tn(uv, bk[u]), 0.0)) * tile(b, j, "p_last")
    for u, (b, j) in enumerate(units):
        s_scr[b, j] = S[u]
    for b in range(nrows):
        o = jnp.concatenate(outs[b], axis=-1)
        mean = head_sums(o) * (1.0 / R_HEAD)
        d = o - mean
        var = head_sums(d * d) * (1.0 / R_HEAD)
        o = d * lax.rsqrt(var + R_GN_EPS)
        o_ref[b] = (o * lnw_ref[...] + lnb_ref[...] + rows[b]["bonus"]) * rows[b]["g"]

    @pl.when(c == nc - 1)
    def _():
        wkv_out_ref[...] = s_scr[...]


def _rwkv(proj_r, shift0, wkv0, wts):
    bsz, t, _ = proj_r.shape
    nc = t // CHUNK
    npairs = R_WIDTH // LANES
    w5 = wkv0.reshape(bsz, npairs, 2, R_HEAD, R_HEAD)
    zero = jnp.zeros_like(w5[:, :, 0])
    wkv_bd = jnp.concatenate([jnp.concatenate([w5[:, :, 0], zero], axis=-1),
                              jnp.concatenate([zero, w5[:, :, 1]], axis=-1)], axis=-2)
    full = lambda shape: pl.BlockSpec(shape, lambda b, c: (0,) * len(shape))
    nb = math.gcd(RWKV_ROWS_PER_STEP, bsz)
    o_r, s_bd = pl.pallas_call(
        _rwkv_kernel,
        grid=(bsz // nb, nc),
        in_specs=[
            pl.BlockSpec((nb, CHUNK, R_PROJ), lambda b, c: (b, c, 0)),
            pl.BlockSpec((nb, 1, R_PROJ), lambda b, c: (b, 0, 0)),
            pl.BlockSpec((nb, npairs, LANES, LANES), lambda b, c: (b, 0, 0, 0)),
            full((1, R_PROJ)), full((1, R_WIDTH)), full((LANES, R_WIDTH)), full((1, R_WIDTH)),
            full((LANES, R_WIDTH)), full((GATE_LORA, R_WIDTH)), full((1, R_WIDTH)), full((1, R_WIDTH)),
            full((1, R_WIDTH)), full((1, R_WIDTH)), full((1, R_WIDTH)),
        ],
        out_specs=[
            pl.BlockSpec((nb, CHUNK, R_WIDTH), lambda b, c: (b, c, 0)),
            pl.BlockSpec((nb, npairs, LANES, LANES), lambda b, c: (b, 0, 0, 0)),
        ],
        out_shape=[
            jax.ShapeDtypeStruct((bsz, t, R_WIDTH), F32),
            jax.ShapeDtypeStruct((bsz, npairs, LANES, LANES), F32),
        ],
        scratch_shapes=[pltpu.VMEM((nb, npairs, LANES, LANES), F32), pltpu.VMEM((nb, SUBLANES, R_PROJ), F32)],
        compiler_params=pltpu.CompilerParams(
            dimension_semantics=("arbitrary", "arbitrary"), vmem_limit_bytes=40 * 1024 * 1024),
        name="rwkv",
    )(proj_r, shift0.reshape(bsz, 1, R_PROJ), wkv_bd, *wts)
    new_wkv = jnp.stack([s_bd[:, :, :R_HEAD, :R_HEAD], s_bd[:, :, R_HEAD:, R_HEAD:]], axis=2)
    return o_r, new_wkv.reshape(bsz, R_HEADS, R_HEAD, R_HEAD)


def _ssd_kernel(p_ref, conv0_ref, ssm0_ref, cw_ref, cb_ref, dtb_ref, alog_ref, dvec_ref, nw_ref,
                o_ref, conv_out_ref, ssm_out_ref, h_scr, prev_scr):
    c = pl.program_id(1)
    nc = pl.num_programs(1)
    L = CHUNK

    @pl.when(c == 0)
    def _():
        prev_scr[...] = conv0_ref[0]
        h_scr[...] = ssm0_ref[0]

    p = p_ref[0]
    z = p[:, 0:M_WIDTH]
    xbc = p[:, M_WIDTH:M_WIDTH + CONV_DIM]
    dt_raw = p[:, M_WIDTH + CONV_DIM:]
    prev8 = prev_scr[...]
    row8 = lax.broadcasted_iota(jnp.int32, (SUBLANES, 1), 0)
    conv = xbc * cw_ref[CONV_W - 1:CONV_W, :]
    for j in range(1, CONV_W):
        xs_j = pltpu.roll(xbc, j, 0)
        first = jnp.where(row8 >= j, xs_j[0:SUBLANES], pltpu.roll(prev8, j, 0))
        shifted = jnp.concatenate([first, xs_j[SUBLANES:]], axis=0)
        conv = conv + shifted * cw_ref[CONV_W - 1 - j:CONV_W - j, :]
    prev_scr[...] = xbc[L - SUBLANES:L, :]
    act = _silu(conv + cb_ref[...])
    xs = act[:, 0:M_WIDTH]
    Bm = act[:, M_WIDTH:M_WIDTH + M_GROUPS * M_STATE]
    Cm = act[:, M_WIDTH + M_GROUPS * M_STATE:]
    dt = _softplus(dt_raw + dtb_ref[...])
    A = -jnp.exp(alog_ref[...])
    ti = lax.broadcasted_iota(jnp.int32, (L, L), 0)
    si = lax.broadcasted_iota(jnp.int32, (L, L), 1)
    causal = si <= ti
    cs = _dot_f32_rhs(causal.astype(BF16), dt * A)
    cs_t = cs.T
    heads = range(M_HEADS)
    group = lambda h: slice((h // (M_HEADS // M_GROUPS)) * M_STATE, (h // (M_HEADS // M_GROUPS) + 1) * M_STATE)
    Bg = [Bm[:, group(h)].astype(BF16) for h in heads]
    Cg = [Cm[:, group(h)].astype(BF16) for h in heads]
    cb = [_bdot_nt(Cm[:, g * M_STATE:(g + 1) * M_STATE], Bm[:, g * M_STATE:(g + 1) * M_STATE]) for g in range(M_GROUPS)]
    hin = [h_scr[h] for h in heads]
    xh = [xs[:, h * M_HEADDIM:(h + 1) * M_HEADDIM] for h in heads]
    xdt = [xh[h] * dt[:, h:h + 1] for h in heads]
    cs_col = [cs[:, h:h + 1] for h in heads]
    cs_last = [cs[L - 1:L, h:h + 1] for h in heads]
    y_off = [_bdot_nt(Cg[h], hin[h]) for h in heads]
    y_diag = []
    for h in heads:
        seg = jnp.where(causal, cs_col[h] - cs_t[h:h + 1, :], 0.0)
        decay = jnp.where(causal, jnp.exp(seg), 0.0)
        y_diag.append(_bdot(cb[h // (M_HEADS // M_GROUPS)] * decay, xdt[h]))
    st = [_bdot_tn(xdt[h] * jnp.exp(cs_last[h] - cs_col[h]), Bg[h]) for h in heads]
    for h in heads:
        h_scr[h] = hin[h] * jnp.exp(cs_last[h]) + st[h]
    ys = [y_diag[h] + jnp.exp(cs_col[h]) * y_off[h] + dvec_ref[:, h:h + 1] * xh[h] for h in heads]
    y = jnp.concatenate(ys, axis=-1) * _silu(z)
    gw = M_WIDTH // M_GROUPS
    parts = []
    for g in range(M_GROUPS):
        yg = y[:, g * gw:(g + 1) * gw]
        parts.append(yg * lax.rsqrt(jnp.mean(yg * yg, axis=-1, keepdims=True) + NORM_EPS))
    o_ref[0] = jnp.concatenate(parts, axis=-1) * nw_ref[...]

    @pl.when(c == nc - 1)
    def _():
        conv_out_ref[0] = prev_scr[...]
        ssm_out_ref[0] = h_scr[...]


def _ssd(proj_m, conv0_8, ssm0, wts):
    bsz, t, _ = proj_m.shape
    nc = t // CHUNK
    full = lambda shape: pl.BlockSpec(shape, lambda b, c: (0,) * len(shape))
    return pl.pallas_call(
        _ssd_kernel,
        grid=(bsz, nc),
        in_specs=[
            pl.BlockSpec((1, CHUNK, M_PROJ_PAD), lambda b, c: (b, c, 0)),
            pl.BlockSpec((1, SUBLANES, CONV_DIM), lambda b, c: (b, 0, 0)),
            pl.BlockSpec((1, M_HEADS, M_HEADDIM, M_STATE), lambda b, c: (b, 0, 0, 0)),
            full((CONV_W, CONV_DIM)), full((1, CONV_DIM)), full((1, LANES)), full((1, LANES)),
            full((1, LANES)), full((1, M_WIDTH)),
        ],
        out_specs=[
            pl.BlockSpec((1, CHUNK, M_WIDTH), lambda b, c: (b, c, 0)),
            pl.BlockSpec((1, SUBLANES, CONV_DIM), lambda b, c: (b, 0, 0)),
            pl.BlockSpec((1, M_HEADS, M_HEADDIM, M_STATE), lambda b, c: (b, 0, 0, 0)),
        ],
        out_shape=[
            jax.ShapeDtypeStruct((bsz, t, M_WIDTH), F32),
            jax.ShapeDtypeStruct((bsz, SUBLANES, CONV_DIM), F32),
            jax.ShapeDtypeStruct((bsz, M_HEADS, M_HEADDIM, M_STATE), F32),
        ],
        scratch_shapes=[pltpu.VMEM((M_HEADS, M_HEADDIM, M_STATE), F32), pltpu.VMEM((SUBLANES, CONV_DIM), F32)],
        compiler_params=pltpu.CompilerParams(
            dimension_semantics=("arbitrary", "arbitrary"), vmem_limit_bytes=40 * 1024 * 1024),
        name="ssd",
    )(proj_m, conv0_8, ssm0, *wts)


def _top16_rows(s, rowid, big):
    vals, ids = [], []
    for _ in range(P_TOPK):
        m = jnp.max(s, axis=0, keepdims=True)
        idx = jnp.min(jnp.where(s == m, rowid, big), axis=0, keepdims=True)
        s = jnp.where(rowid == idx, -jnp.inf, s)
        vals.append(m)
        ids.append(idx)
    return vals, ids


def _route_kernel(or_ref, om_ref, x_ref, wr_ref, wm_ref, n2_ref, wq_ref, keys_ref,
                  h_ref, xn_ref, off_ref, gate_ref, sv_scr, si_scr):
    tm = x_ref.shape[0]
    h = (x_ref[...]
         + jnp.dot(or_ref[...].astype(BF16), wr_ref[...], preferred_element_type=F32)
         + jnp.dot(om_ref[...].astype(BF16), wm_ref[...], preferred_element_type=F32))
    h_ref[...] = h
    xn = h * lax.rsqrt(jnp.mean(h * h, axis=-1, keepdims=True) + NORM_EPS) * n2_ref[...]
    xn_ref[...] = xn
    xn_b = xn.astype(BF16)
    rowid = lax.broadcasted_iota(jnp.int32, (N_KEYS, tm), 0)

    def sub_body(hd, carry):
        res = []
        for half in range(2):
            q = jnp.dot(xn_b, wq_ref[2 * hd + half], preferred_element_type=F32)
            s_t = _dot_nt(keys_ref[half], q)
            res.append(_top16_rows(s_t, rowid, N_KEYS))
        for half, (vals, ids) in enumerate(res):
            sv_scr[2 * hd + half] = jnp.concatenate(vals, axis=0)
            si_scr[2 * hd + half] = jnp.concatenate(ids, axis=0)
        return carry

    lax.fori_loop(0, P_HEADS, sub_body, 0)

    r16 = lax.broadcasted_iota(jnp.int32, (16, 1), 0)
    r8 = lax.broadcasted_iota(jnp.int32, (8, 1), 0)

    def head_body(hd, carry):
        sv0, sv1 = sv_scr[2 * hd], sv_scr[2 * hd + 1]
        si0, si1 = si_scr[2 * hd], si_scr[2 * hd + 1]
        cand, cid, flat = [], [], []
        for i in range(8):
            n_i = P_TOPK // (i + 1)
            rows = 16 if i == 0 else 8
            rr = r16 if i == 0 else r8
            cand.append(jnp.where(rr < n_i, sv0[i:i + 1] + sv1[0:rows], -jnp.inf))
            cid.append(si0[i:i + 1] * N_KEYS + si1[0:rows])
            flat.append(jnp.broadcast_to(i * P_TOPK + rr, (rows, tm)))
        cand.append(sv0[8:16] + sv1[0:1])
        cid.append(si0[8:16] * N_KEYS + si1[0:1])
        flat.append(jnp.broadcast_to((8 + r8) * P_TOPK, (8, tm)))
        cand = jnp.concatenate(cand, axis=0)
        cid = jnp.concatenate(cid, axis=0)
        flat = jnp.concatenate(flat, axis=0)
        tops, eids = [], []
        for _ in range(P_TOPK):
            m = jnp.max(cand, axis=0, keepdims=True)
            sel = jnp.min(jnp.where(cand == m, flat, 4 * P_TOPK * P_TOPK), axis=0, keepdims=True)
            hit = flat == sel
            eids.append(jnp.sum(jnp.where(hit, cid, 0), axis=0, keepdims=True))
            cand = jnp.where(hit, -jnp.inf, cand)
            tops.append(m)
        top_s = jnp.concatenate(tops, axis=0)
        eid = jnp.concatenate(eids, axis=0)
        ex = jnp.exp(top_s - top_s[0:1])
        gate = ex / jnp.sum(ex, axis=0, keepdims=True)
        si_scr[hd] = TAB_PAD + TAB_ROWS_PER_EXPERT * eid
        sv_scr[hd] = gate
        return carry

    lax.fori_loop(0, P_HEADS, head_body, 0)
    off_ref[...] = jnp.concatenate([si_scr[hd] for hd in range(P_HEADS)], axis=0).T
    gate_ref[...] = jnp.concatenate([sv_scr[hd] for hd in range(P_HEADS)], axis=0).T


def _route(o_r, o_m, x2d, w_out_r, w_out_m, norm2_w, wq3, keys, tm=512):
    n = x2d.shape[0]
    tm = math.gcd(tm, n)
    full = lambda shape: pl.BlockSpec(shape, lambda i: (0,) * len(shape))
    return pl.pallas_call(
        _route_kernel,
        grid=(n // tm,),
        in_specs=[
            pl.BlockSpec((tm, R_WIDTH), lambda i: (i, 0)),
            pl.BlockSpec((tm, M_WIDTH), lambda i: (i, 0)),
            pl.BlockSpec((tm, D_MODEL), lambda i: (i, 0)),
            full((R_WIDTH, D_MODEL)), full((M_WIDTH, D_MODEL)), full((1, D_MODEL)),
            full((2 * P_HEADS, D_MODEL, P_QDIM // 2)), full((2, N_KEYS, P_QDIM // 2)),
        ],
        out_specs=[
            pl.BlockSpec((tm, D_MODEL), lambda i: (i, 0)),
            pl.BlockSpec((tm, D_MODEL), lambda i: (i, 0)),
            pl.BlockSpec((tm, PAIRS), lambda i: (i, 0)),
            pl.BlockSpec((tm, PAIRS), lambda i: (i, 0)),
        ],
        out_shape=[
            jax.ShapeDtypeStruct((n, D_MODEL), F32),
            jax.ShapeDtypeStruct((n, D_MODEL), F32),
            jax.ShapeDtypeStruct((n, PAIRS), jnp.int32),
            jax.ShapeDtypeStruct((n, PAIRS), F32),
        ],
        scratch_shapes=[pltpu.VMEM((2 * P_HEADS, P_TOPK, tm), F32), pltpu.VMEM((2 * P_HEADS, P_TOPK, tm), jnp.int32)],
        compiler_params=pltpu.CompilerParams(
            dimension_semantics=("arbitrary",), vmem_limit_bytes=48 * 1024 * 1024),
        name="route",
    )(o_r, o_m, x2d, w_out_r, w_out_m, norm2_w, wq3, keys)


def _pack_table(tab):
    tb = tab.astype(BF16)
    half = D_MODEL // 2
    lo = lax.bitcast_convert_type(tb[:, :half], jnp.uint16).astype(jnp.uint32)
    hi = lax.bitcast_convert_type(tb[:, half:], jnp.uint16).astype(jnp.uint32)
    w = (lo | (hi << 16)).reshape(N_EXPERTS * TAB_ROWS_PER_EXPERT, LANES)
    return jnp.pad(w, ((TAB_PAD, TAB_PAD), (0, 0)))


V_ROWS = 2 * SUBLANES
PEER_TOKENS_PER_STEP = 8
V_KCHUNK = 256
HID_PAD = SUBLANES


def _merged_words(tab_vmem, offs, p_lo, p_hi, low):
    wa = tab_vmem[pl.ds(offs[p_lo], SUBLANES), :]
    wb = tab_vmem[pl.ds(offs[p_hi] - TAB_ROWS_PER_EXPERT, SUBLANES), :]
    return jnp.where(low, wa, wb)


def _peer_u_kernel(off_ref, x3_ref, gate_ref, tab_vmem, c_ref, r_scr, hid_scr):
    tb = x3_ref.shape[0]
    nsteps = tb // PEER_TOKENS_PER_STEP
    row = lax.broadcasted_iota(jnp.int32, (SUBLANES, LANES), 0)
    low = row < 4
    m2 = (row % 4) < 2
    m1 = (row % 2) == 0
    ones_b = jnp.ones((SUBLANES, LANES), BF16)

    def fold(a, b, sh, m):
        return jnp.where(m, a + pltpu.roll(a, SUBLANES - sh, 0), b + pltpu.roll(b, sh, 0))

    def flush_lane_sums(step):
        for j in range(PEER_TOKENS_PER_STEP):
            r = r_scr[j]
            hi = r.astype(BF16)
            lo = (r - hi.astype(F32)).astype(BF16)
            sums = (lax.dot_general(ones_b, hi, (((1,), (1,)), ((), ())), preferred_element_type=F32)
                    + lax.dot_general(ones_b, lo, (((1,), (1,)), ((), ())), preferred_element_type=F32))
            hid_scr[pl.ds(step * PEER_TOKENS_PER_STEP + j + HID_PAD - PEER_TOKENS_PER_STEP, 1), :] = sums[0:1, :]

    r_scr[...] = jnp.zeros(r_scr.shape, F32)

    def group_body(tg, carry):
        flush_lane_sums(tg)
        toks = [tg * PEER_TOKENS_PER_STEP + j for j in range(PEER_TOKENS_PER_STEP)]
        offs = [off_ref.at[t] for t in toks]
        xa, xb = [], []
        for t in toks:
            x = x3_ref[t]
            xr = pltpu.roll(x, 4, 0)
            xa.append(jnp.where(low, x, xr))
            xb.append(jnp.where(low, xr, x))
        for g in range(PAIRS // 8):
            b = g * 8
            for j in range(PEER_TOKENS_PER_STEP):
                def prod(p_lo, p_hi):
                    w = _merged_words(tab_vmem, offs[j], p_lo, p_hi, low)
                    lo = pltpu.bitcast(w << 16, F32)
                    hi = pltpu.bitcast(w & jnp.uint32(0xFFFF0000), F32)
                    return lo * xa[j] + hi * xb[j]
                n1 = fold(prod(b + 0, b + 4), prod(b + 2, b + 6), 2, m2)
                n2 = fold(prod(b + 1, b + 5), prod(b + 3, b + 7), 2, m2)
                r_scr[j, pl.ds(b, 8), :] = fold(n1, n2, 1, m1)
        return carry

    lax.fori_loop(0, nsteps, group_body, 0)
    flush_lane_sums(nsteps)
    hid = hid_scr[pl.ds(HID_PAD, tb), :]
    gelu = 0.5 * hid * (1.0 + lax.erf(hid * (1.0 / math.sqrt(2.0))))
    c_ref[...] = gelu * gate_ref[...]


def _peer_u(off, x3, gate, tab, tb=128):
    n = x3.shape[0]
    tb = math.gcd(tb, n)
    return pl.pallas_call(
        _peer_u_kernel,
        grid=(n // tb,),
        in_specs=[
            pl.BlockSpec((tb, PAIRS), lambda i: (i, 0), memory_space=pltpu.SMEM),
            pl.BlockSpec((tb, SUBLANES, LANES), lambda i: (i, 0, 0)),
            pl.BlockSpec((tb, PAIRS), lambda i: (i, 0)),
            pl.BlockSpec(memory_space=pltpu.VMEM),
        ],
        out_specs=pl.BlockSpec((tb, PAIRS), lambda i: (i, 0)),
        out_shape=jax.ShapeDtypeStruct((n, PAIRS), F32),
        scratch_shapes=[pltpu.VMEM((PEER_TOKENS_PER_STEP, PAIRS, LANES), F32),
                        pltpu.VMEM((tb + HID_PAD, PAIRS), F32)],
        compiler_params=pltpu.CompilerParams(
            dimension_semantics=("arbitrary",), vmem_limit_bytes=56 * 1024 * 1024),
        name="peer_u",
    )(off, x3, gate, tab)


def _peer_v_kernel(off_ref, c_ref, h3_ref, fw_ref, tab_vmem, y_ref, crep_scr):
    tb = h3_ref.shape[0]
    kw = PAIRS * SUBLANES
    fw = fw_ref[...]
    low = lax.broadcasted_iota(jnp.int32, (SUBLANES, LANES), 0) < 4
    rp = lax.broadcasted_iota(jnp.int32, (PAIRS, kw), 0)
    rc = lax.broadcasted_iota(jnp.int32, (PAIRS, kw), 1)
    crep_scr[...] = _dot_f32_lhs(c_ref[...], (rc // SUBLANES == rp).astype(BF16))
    m = lax.broadcasted_iota(jnp.int32, (SUBLANES, kw), 0)
    col = lax.broadcasted_iota(jnp.int32, (SUBLANES, kw), 1)
    sel = m == (col % 2) * 4 + ((col % V_ROWS) // 2) % 4

    def group_body(tg, carry):
        toks = [tg * PEER_TOKENS_PER_STEP + j for j in range(PEER_TOKENS_PER_STEP)]
        offs = [off_ref.at[t] for t in toks]
        x16 = []
        for t in toks:
            lhs = jnp.where(sel, crep_scr[pl.ds(t, 1), :], 0.0)
            hi = pltpu.bitcast(pltpu.bitcast(lhs, jnp.uint32) & jnp.uint32(0xFFFF0000), F32)
            x16.append(jnp.concatenate([hi, lhs - hi], axis=0).astype(BF16))
        acc = [None] * len(toks)
        per_chunk = V_KCHUNK // V_ROWS
        for kc in range(kw // V_KCHUNK):
            for j in range(len(toks)):
                tiles = [pltpu.bitcast(_merged_words(tab_vmem, offs[j], 2 * mm, 2 * mm + 1, low), BF16)
                         for mm in range(kc * per_chunk, (kc + 1) * per_chunk)]
                part = jnp.dot(x16[j][:, kc * V_KCHUNK:(kc + 1) * V_KCHUNK], jnp.concatenate(tiles, axis=0),
                               preferred_element_type=F32)
                acc[j] = part if kc == 0 else acc[j] + part
        for j, t in enumerate(toks):
            y_ref[t] = acc[j][0:SUBLANES] + acc[j][SUBLANES:]
        return carry

    lax.fori_loop(0, tb // PEER_TOKENS_PER_STEP, group_body, 0)
    hn = h3_ref[...] + y_ref[...]
    ms = jnp.sum(jnp.sum(hn * hn, axis=2, keepdims=True), axis=1, keepdims=True) * (1.0 / D_MODEL)
    y_ref[...] = hn * lax.rsqrt(ms + NORM_EPS) * fw


def _peer_v(off, c, h3, fw, tab, tb=128):
    n = h3.shape[0]
    tb = math.gcd(tb, n)
    return pl.pallas_call(
        _peer_v_kernel,
        grid=(n // tb,),
        in_specs=[
            pl.BlockSpec((tb, PAIRS), lambda i: (i, 0), memory_space=pltpu.SMEM),
            pl.BlockSpec((tb, PAIRS), lambda i: (i, 0)),
            pl.BlockSpec((tb, SUBLANES, LANES), lambda i: (i, 0, 0)),
            pl.BlockSpec((SUBLANES, LANES), lambda i: (0, 0)),
            pl.BlockSpec(memory_space=pltpu.VMEM),
        ],
        out_specs=pl.BlockSpec((tb, SUBLANES, LANES), lambda i: (i, 0, 0)),
        out_shape=jax.ShapeDtypeStruct((n, SUBLANES, LANES), F32),
        scratch_shapes=[pltpu.VMEM((tb, PAIRS * SUBLANES), F32)],
        compiler_params=pltpu.CompilerParams(
            dimension_semantics=("arbitrary",), vmem_limit_bytes=56 * 1024 * 1024),
        name="peer_v",
    )(off, c, h3, fw, tab)


def _prep_weights(norm1_w, w_in, rwkv_mu, rwkv_w0, rwkv_w2, rwkv_a0, rwkv_a2, rwkv_g2, rwkv_k_k, rwkv_k_a,
                  rwkv_r_k, rwkv_ln_w, rwkv_ln_b, mamba_conv_w, mamba_conv_b, mamba_dt_bias, mamba_A_log,
                  mamba_D, mamba_norm_w, w_out, norm2_w, peer_w_q, peer_sub_keys, peer_u, peer_v, final_norm_w):
    row = lambda a: a.reshape(1, -1)
    pad_lanes = lambda a: jnp.pad(row(a), ((0, 0), (0, LANES - a.shape[-1])))
    w_cat = jnp.pad(w_in, ((0, 0), (0, M_PROJ_PAD - M_PROJ))).astype(BF16)
    zeros = jnp.zeros((DECAY_LORA, R_WIDTH), F32)
    rw = (row(rwkv_mu), row(rwkv_w0), jnp.concatenate([rwkv_w2, zeros], 0), row(rwkv_a0),
          jnp.concatenate([zeros, rwkv_a2], 0), rwkv_g2, row(rwkv_k_k), row(rwkv_k_a), row(rwkv_r_k),
          row(rwkv_ln_w), row(rwkv_ln_b))
    mw = (mamba_conv_w, row(mamba_conv_b), pad_lanes(mamba_dt_bias), pad_lanes(mamba_A_log),
          pad_lanes(mamba_D), row(mamba_norm_w))
    wq3 = peer_w_q.reshape(D_MODEL, 2 * P_HEADS, P_QDIM // 2).transpose(1, 0, 2).astype(BF16)
    route_w = (w_out[:R_WIDTH].astype(BF16), w_out[R_WIDTH:].astype(BF16), row(norm2_w), wq3, peer_sub_keys)
    return dict(norm1=row(norm1_w), w_cat=w_cat, rw=rw, mw=mw, route_w=route_w,
                tab_u=_pack_table(peer_u), tab_v=_pack_table(peer_v),
                fw=final_norm_w.reshape(SUBLANES, LANES))


def _trunk(x, shift0, wkv0, conv0, ssm0, w):
    bsz, t, _ = x.shape
    n = bsz * t
    x2d = x.reshape(n, D_MODEL)
    proj_r, proj_m = _in_proj(x2d, w["norm1"], w["w_cat"])
    proj_r = proj_r.reshape(bsz, t, R_PROJ)
    proj_m = proj_m.reshape(bsz, t, M_PROJ_PAD)
    o_r, new_wkv = _rwkv(proj_r, shift0, wkv0, w["rw"])
    conv0_8 = jnp.pad(conv0, ((0, 0), (SUBLANES - (CONV_W - 1), 0), (0, 0)))
    o_m, new_conv8, new_ssm = _ssd(proj_m, conv0_8, ssm0, w["mw"])
    h, xn, off, gate = _route(o_r.reshape(n, R_WIDTH), o_m.reshape(n, M_WIDTH), x2d, *w["route_w"])
    c = _peer_u(off, xn.reshape(n, SUBLANES, LANES), gate, w["tab_u"])
    y = _peer_v(off, c, h.reshape(n, SUBLANES, LANES), w["fw"], w["tab_v"])
    new_shift = proj_r[:, -1, :]
    new_conv = new_conv8[:, SUBLANES - (CONV_W - 1):, :]
    return y.reshape(bsz, t, D_MODEL), new_shift[None], new_wkv[None], new_conv[None], new_ssm[None]


def kernel(x_prompt, x_sample, state_rwkv_shift, state_rwkv_wkv, state_mamba_conv, state_mamba_ssm, norm1_w, w_in, rwkv_mu, rwkv_w0, rwkv_w2, rwkv_a0, rwkv_a2, rwkv_g2, rwkv_k_k, rwkv_k_a, rwkv_r_k, rwkv_ln_w, rwkv_ln_b, mamba_conv_w, mamba_conv_b, mamba_dt_bias, mamba_A_log, mamba_D, mamba_norm_w, w_out, norm2_w, peer_w_q, peer_sub_keys, peer_u, peer_v, final_norm_w):
    w = _prep_weights(norm1_w[0], w_in[0], rwkv_mu[0], rwkv_w0[0], rwkv_w2[0], rwkv_a0[0], rwkv_a2[0], rwkv_g2[0],
                      rwkv_k_k[0], rwkv_k_a[0], rwkv_r_k[0].reshape(-1), rwkv_ln_w[0], rwkv_ln_b[0],
                      mamba_conv_w[0], mamba_conv_b[0], mamba_dt_bias[0], mamba_A_log[0], mamba_D[0],
                      mamba_norm_w[0], w_out[0], norm2_w[0], peer_w_q[0], peer_sub_keys[0], peer_u[0], peer_v[0],
                      final_norm_w)
    bp = x_prompt.shape[0]
    zp = lambda *s: jnp.zeros((bp,) + s, F32)
    yp, *sp = _trunk(x_prompt, zp(R_PROJ), zp(R_HEADS, R_HEAD, R_HEAD), zp(CONV_W - 1, CONV_DIM),
                     zp(M_HEADS, M_HEADDIM, M_STATE), w)
    ys, *ss = _trunk(x_sample, state_rwkv_shift[0], state_rwkv_wkv[0], state_mamba_conv[0], state_mamba_ssm[0], w)
    return (yp, ys, *sp, *ss)
```

```python
import functools
import math

import jax
import jax.numpy as jnp
import numpy as np
from jax import lax
from jax.experimental import pallas as pl
from jax.experimental.pallas import tpu as pltpu

LANES = 128
SUBLANES = 8
VMEM_BYTES_V7X = 64 * 1024 * 1024

D_MODEL = 1024
CHUNK = 64
R_WIDTH = 512
R_HEAD = 64
R_HEADS = R_WIDTH // R_HEAD
DECAY_LORA = 64
AAA_LORA = 64
GATE_LORA = 128
R_PROJ = 3 * R_WIDTH + DECAY_LORA + AAA_LORA + GATE_LORA
R_GN_EPS = R_HEAD * 1e-5
M_WIDTH = 512
M_HEADDIM = 64
M_HEADS = M_WIDTH // M_HEADDIM
M_GROUPS = 2
M_STATE = 64
CONV_W = 4
CONV_DIM = M_WIDTH + 2 * M_GROUPS * M_STATE
M_PROJ = M_WIDTH + CONV_DIM + M_HEADS
M_PROJ_PAD = M_WIDTH + CONV_DIM + LANES
N_KEYS = 128
N_EXPERTS = N_KEYS * N_KEYS
P_HEADS = 8
P_QDIM = 256
P_TOPK = 16
PAIRS = P_HEADS * P_TOPK
NORM_EPS = 1e-6

HI = lax.Precision.HIGHEST
F32 = jnp.float32
BF16 = jnp.bfloat16

TAB_ROWS_PER_EXPERT = 4
TAB_PAD = 8
TAB_ROWS = N_EXPERTS * TAB_ROWS_PER_EXPERT + 2 * TAB_PAD


def _dot(a, b, precision=HI):
    return jnp.dot(a, b, precision=precision, preferred_element_type=F32)


def _dot_nt(a, b, precision=HI):
    return lax.dot_general(a, b, (((1,), (1,)), ((), ())), precision=precision, preferred_element_type=F32)


def _dot_tn(a, b, precision=HI):
    return lax.dot_general(a, b, (((0,), (0,)), ((), ())), precision=precision, preferred_element_type=F32)


def _softplus(x):
    return jnp.maximum(x, 0.0) + jnp.log(1.0 + jnp.exp(-jnp.abs(x)))


def _sigmoid(x):
    return 1.0 / (1.0 + jnp.exp(-x))


def _silu(x):
    return x * _sigmoid(x)


def _in_proj_kernel(x_ref, nw_ref, w_ref, pr_ref, pm_ref):
    x = x_ref[...]
    xn = x * lax.rsqrt(jnp.mean(x * x, axis=-1, keepdims=True) + NORM_EPS) * nw_ref[...]
    p = jnp.dot(xn.astype(BF16), w_ref[...], preferred_element_type=F32)
    pr_ref[...] = p[:, :R_PROJ]
    pm_ref[...] = p[:, R_PROJ:]


def _in_proj(x2d, norm_w, w_cat_bf16, tm=512):
    n = x2d.shape[0]
    tm = math.gcd(tm, n)
    wtot = R_PROJ + M_PROJ_PAD
    return pl.pallas_call(
        _in_proj_kernel,
        grid=(n // tm,),
        in_specs=[
            pl.BlockSpec((tm, D_MODEL), lambda i: (i, 0)),
            pl.BlockSpec((1, D_MODEL), lambda i: (0, 0)),
            pl.BlockSpec((D_MODEL, wtot), lambda i: (0, 0)),
        ],
        out_specs=[
            pl.BlockSpec((tm, R_PROJ), lambda i: (i, 0)),
            pl.BlockSpec((tm, M_PROJ_PAD), lambda i: (i, 0)),
        ],
        out_shape=[
            jax.ShapeDtypeStruct((n, R_PROJ), F32),
            jax.ShapeDtypeStruct((n, M_PROJ_PAD), F32),
        ],
        compiler_params=pltpu.CompilerParams(
            dimension_semantics=("arbitrary",), vmem_limit_bytes=48 * 1024 * 1024),
        name="in_proj",
    )(x2d, norm_w, w_cat_bf16)


def _split3(x):
    hi = x.astype(BF16)
    r1 = x - hi.astype(F32)
    mid = r1.astype(BF16)
    lo = (r1 - mid.astype(F32)).astype(BF16)
    return hi, mid, lo


def _dot_f32_lhs(x, b_exact):
    return sum(jnp.dot(piece, b_exact, preferred_element_type=F32) for piece in _split3(x))


def _dot_f32_rhs(a_exact, x):
    return sum(jnp.dot(a_exact, piece, preferred_element_type=F32) for piece in _split3(x))


def _dot_x3(a, b):
    a_hi = a.astype(BF16)
    a_lo = (a - a_hi.astype(F32)).astype(BF16)
    b_hi = b.astype(BF16)
    b_lo = (b - b_hi.astype(F32)).astype(BF16)
    d = lambda x, y: jnp.dot(x, y, preferred_element_type=F32)
    return d(a_hi, b_hi) + d(a_hi, b_lo) + d(a_lo, b_hi)


def _bdot(a, b):
    return jnp.dot(a.astype(BF16), b.astype(BF16), preferred_element_type=F32)


def _bdot_nt(a, b):
    return lax.dot_general(a.astype(BF16), b.astype(BF16), (((1,), (1,)), ((), ())), preferred_element_type=F32)


def _bdot_tn(a, b):
    return lax.dot_general(a.astype(BF16), b.astype(BF16), (((0,), (0,)), ((), ())), preferred_element_type=F32)


RWKV_ROWS_PER_STEP = 2


def _rwkv_kernel(p_ref, shift0_ref, wkv0_ref, mu_ref, w0_ref, w2_ref, a0_ref, a2_ref, g2_ref, kk_ref, ka_ref,
                 rk_ref, lnw_ref, lnb_ref, o_ref, wkv_out_ref, s_scr, prev_scr):
    c = pl.program_id(1)
    nc = pl.num_programs(1)
    L = CHUNK
    npairs = R_WIDTH // LANES
    nrows = p_ref.shape[0]

    @pl.when(c == 0)
    def _():
        for b in range(nrows):
            prev_scr[b] = jnp.broadcast_to(shift0_ref[b], prev_scr.shape[1:])
        s_scr[...] = wkv0_ref[...]

    li = lax.broadcasted_iota(jnp.int32, (LANES, LANES), 0)
    lj = lax.broadcasted_iota(jnp.int32, (LANES, LANES), 1)
    bd_mask = (li // R_HEAD) == (lj // R_HEAD)
    seg = bd_mask.astype(BF16)

    def head_sums(x):
        return jnp.concatenate([_dot_f32_lhs(x[:, LANES * j:LANES * (j + 1)], seg) for j in range(npairs)], axis=-1)

    ti = lax.broadcasted_iota(jnp.int32, (L, L), 0)
    si = lax.broadcasted_iota(jnp.int32, (L, L), 1)
    tri = (si <= ti).astype(BF16)
    row = lax.broadcasted_iota(jnp.int32, (L, 1), 0)

    rows = []
    for b in range(nrows):
        p = p_ref[b]
        p_prev = jnp.where(row == 0, prev_scr[b, 0:1, :], pltpu.roll(p, 1, 0))
        prev_scr[b] = jnp.broadcast_to(p[L - 1:L, :], prev_scr.shape[1:])
        pm = p + (p_prev - p) * mu_ref[...]
        r = pm[:, 0:R_WIDTH]
        k = pm[:, R_WIDTH:2 * R_WIDTH]
        v = pm[:, 2 * R_WIDTH:3 * R_WIDTH]
        wa = pm[:, 3 * R_WIDTH:3 * R_WIDTH + LANES]
        gd = pm[:, 3 * R_WIDTH + LANES:3 * R_WIDTH + 2 * LANES]
        w_log = -_softplus(-(w0_ref[...] + _dot_x3(jnp.tanh(wa), w2_ref[...]))) - 0.5
        lw = -jnp.exp(w_log)
        a = _sigmoid(a0_ref[...] + _dot_x3(wa, a2_ref[...]))
        g = _dot_x3(_sigmoid(gd), g2_ref[...])
        kk = k * kk_ref[...]
        kk = kk / jnp.maximum(jnp.sqrt(head_sums(kk * kk)), 1e-12)
        k_mod = k * (1.0 + (a - 1.0) * ka_ref[...])
        cl = _dot_f32_rhs(tri, lw)
        e_in = jnp.exp(cl)
        e_neg = jnp.exp(-cl)
        rows.append(dict(At=-kk * jnp.exp(cl - lw), Bt=kk * a * e_neg, Kt=k_mod * e_neg, Rt=r * e_in, v=v,
                         p_last=e_in[L - 1:L, :], bonus=head_sums(r * k_mod * rk_ref[...]) * v, g=g))

    lane = lax.broadcasted_iota(jnp.int32, (L, LANES), 1)
    trow = lax.broadcasted_iota(jnp.int32, (L, LANES), 0)
    first = lane < R_HEAD
    strict = (lane % R_HEAD) < trow
    incl = (lane % R_HEAD) <= trow
    zeros = jnp.zeros((L, LANES), F32)

    units = [(b, j) for b in range(nrows) for j in range(npairs)]
    tile = lambda b, j, name: rows[b][name][:, LANES * j:LANES * (j + 1)]
    S = [s_scr[b, j] for b, j in units]
    bk = [jnp.concatenate([tile(b, j, "Bt"), tile(b, j, "Kt")], axis=0).astype(BF16) for b, j in units]
    G = []
    for u, (b, j) in enumerate(units):
        A_, R_ = tile(b, j, "At"), tile(b, j, "Rt")
        x4 = jnp.concatenate([jnp.where(first, A_, 0.0), jnp.where(first, 0.0, A_),
                              jnp.where(first, R_, 0.0), jnp.where(first, 0.0, R_)], axis=0)
        G.append(_bdot_nt(x4, bk[u]))
    ss = [_bdot_nt(jnp.concatenate([tile(b, j, "At"), tile(b, j, "Rt")], axis=0), S[u])
          for u, (b, j) in enumerate(units)]
    m0 = [jnp.where(strict, g_[0:L], 0.0) for g_ in G]
    m1 = [jnp.where(strict, g_[L:2 * L], 0.0) for g_ in G]
    x = []
    for u, (b, j) in enumerate(units):
        zv = jnp.concatenate([zeros, tile(b, j, "v")], axis=0).astype(BF16)
        x.append(ss[u][0:L] + jnp.where(first, _bdot(m0[u], zv), _bdot(m1[u], zv)))
    n0 = [m[:, 0:R_HEAD].astype(BF16) for m in m0]
    n1 = [m[:, 0:R_HEAD].astype(BF16) for m in m1]
    for i in range(6):
        for u in range(len(units)):
            xb = x[u].astype(BF16)
            x[u] = x[u] + jnp.where(first, _bdot(n0[u], xb), _bdot(n1[u], xb))
        if i < 5:
            n0 = [_bdot(n, n).astype(BF16) for n in n0]
            n1 = [_bdot(n, n).astype(BF16) for n in n1]
    outs = [[None] * npairs for _ in range(nrows)]
    for u, (b, j) in enumerate(units):
        uv = jnp.concatenate([x[u], tile(b, j, "v")], axis=0).astype(BF16)
        mo0 = jnp.where(incl, G[u][2 * L:3 * L], 0.0)
        mo1 = jnp.where(incl, G[u][3 * L:4 * L], 0.0)
        outs[b][j] = ss[u][L:2 * L] + jnp.where(first, _bdot(mo0, uv), _bdot(mo1, uv))
        S[u] = (S[u] + jnp.where(bd_mask, _bdot_tn(uv, bk[u]), 0.0)) * tile(b, j, "p_last")
    for u, (b, j) in enumerate(units):
        s_scr[b, j] = S[u]
    for b in range(nrows):
        o = jnp.concatenate(outs[b], axis=-1)
        mean = head_sums(o) * (1.0 / R_HEAD)
        d = o - mean
        var = head_sums(d * d) * (1.0 / R_HEAD)
        o = d * lax.rsqrt(var + R_GN_EPS)
        o_ref[b] = (o * lnw_ref[...] + lnb_ref[...] + rows[b]["bonus"]) * rows[b]["g"]

    @pl.when(c == nc - 1)
    def _():
        wkv_out_ref[...] = s_scr[...]


def _rwkv(proj_r, shift0, wkv0, wts):
    bsz, t, _ = proj_r.shape
    nc = t // CHUNK
    npairs = R_WIDTH // LANES
    w5 = wkv0.reshape(bsz, npairs, 2, R_HEAD, R_HEAD)
    zero = jnp.zeros_like(w5[:, :, 0])
    wkv_bd = jnp.concatenate([jnp.concatenate([w5[:, :, 0], zero], axis=-1),
                              jnp.concatenate([zero, w5[:, :, 1]], axis=-1)], axis=-2)
    full = lambda shape: pl.BlockSpec(shape, lambda b, c: (0,) * len(shape))
    nb = math.gcd(RWKV_ROWS_PER_STEP, bsz)
    o_r, s_bd = pl.pallas_call(
        _rwkv_kernel,
        grid=(bsz // nb, nc),
        in_specs=[
            pl.BlockSpec((nb, CHUNK, R_PROJ), lambda b, c: (b, c, 0)),
            pl.BlockSpec((nb, 1, R_PROJ), lambda b, c: (b, 0, 0)),
            pl.BlockSpec((nb, npairs, LANES, LANES), lambda b, c: (b, 0, 0, 0)),
            full((1, R_PROJ)), full((1, R_WIDTH)), full((LANES, R_WIDTH)), full((1, R_WIDTH)),
            full((LANES, R_WIDTH)), full((GATE_LORA, R_WIDTH)), full((1, R_WIDTH)), full((1, R_WIDTH)),
            full((1, R_WIDTH)), full((1, R_WIDTH)), full((1, R_WIDTH)),
        ],
        out_specs=[
            pl.BlockSpec((nb, CHUNK, R_WIDTH), lambda b, c: (b, c, 0)),
            pl.BlockSpec((nb, npairs, LANES, LANES), lambda b, c: (b, 0, 0, 0)),
        ],
        out_shape=[
            jax.ShapeDtypeStruct((bsz, t, R_WIDTH), F32),
            jax.ShapeDtypeStruct((bsz, npairs, LANES, LANES), F32),
        ],
        scratch_shapes=[pltpu.VMEM((nb, npairs, LANES, LANES), F32), pltpu.VMEM((nb, SUBLANES, R_PROJ), F32)],
        compiler_params=pltpu.CompilerParams(
            dimension_semantics=("arbitrary", "arbitrary"), vmem_limit_bytes=40 * 1024 * 1024),
        name="rwkv",
    )(proj_r, shift0.reshape(bsz, 1, R_PROJ), wkv_bd, *wts)
    new_wkv = jnp.stack([s_bd[:, :, :R_HEAD, :R_HEAD], s_bd[:, :, R_HEAD:, R_HEAD:]], axis=2)
    return o_r, new_wkv.reshape(bsz, R_HEADS, R_HEAD, R_HEAD)


def _ssd_kernel(p_ref, conv0_ref, ssm0_ref, cw_ref, cb_ref, dtb_ref, alog_ref, dvec_ref, nw_ref,
                o_ref, conv_out_ref, ssm_out_ref, h_scr, prev_scr):
    c = pl.program_id(1)
    nc = pl.num_programs(1)
    L = CHUNK

    @pl.when(c == 0)
    def _():
        prev_scr[...] = conv0_ref[0]
        h_scr[...] = ssm0_ref[0]

    p = p_ref[0]
    z = p[:, 0:M_WIDTH]
    xbc = p[:, M_WIDTH:M_WIDTH + CONV_DIM]
    dt_raw = p[:, M_WIDTH + CONV_DIM:]
    prev8 = prev_scr[...]
    row8 = lax.broadcasted_iota(jnp.int32, (SUBLANES, 1), 0)
    conv = xbc * cw_ref[CONV_W - 1:CONV_W, :]
    for j in range(1, CONV_W):
        xs_j = pltpu.roll(xbc, j, 0)
        first = jnp.where(row8 >= j, xs_j[0:SUBLANES], pltpu.roll(prev8, j, 0))
        shifted = jnp.concatenate([first, xs_j[SUBLANES:]], axis=0)
        conv = conv + shifted * cw_ref[CONV_W - 1 - j:CONV_W - j, :]
    prev_scr[...] = xbc[L - SUBLANES:L, :]
    act = _silu(conv + cb_ref[...])
    xs = act[:, 0:M_WIDTH]
    Bm = act[:, M_WIDTH:M_WIDTH + M_GROUPS * M_STATE]
    Cm = act[:, M_WIDTH + M_GROUPS * M_STATE:]
    dt = _softplus(dt_raw + dtb_ref[...])
    A = -jnp.exp(alog_ref[...])
    ti = lax.broadcasted_iota(jnp.int32, (L, L), 0)
    si = lax.broadcasted_iota(jnp.int32, (L, L), 1)
    causal = si <= ti
    cs = _dot_f32_rhs(causal.astype(BF16), dt * A)
    cs_t = cs.T
    heads = range(M_HEADS)
    group = lambda h: slice((h // (M_HEADS // M_GROUPS)) * M_STATE, (h // (M_HEADS // M_GROUPS) + 1) * M_STATE)
    Bg = [Bm[:, group(h)].astype(BF16) for h in heads]
    Cg = [Cm[:, group(h)].astype(BF16) for h in heads]
    cb = [_bdot_nt(Cm[:, g * M_STATE:(g + 1) * M_STATE], Bm[:, g * M_STATE:(g + 1) * M_STATE]) for g in range(M_GROUPS)]
    hin = [h_scr[h] for h in heads]
    xh = [xs[:, h * M_HEADDIM:(h + 1) * M_HEADDIM] for h in heads]
    xdt = [xh[h] * dt[:, h:h + 1] for h in heads]
    cs_col = [cs[:, h:h + 1] for h in heads]
    cs_last = [cs[L - 1:L, h:h + 1] for h in heads]
    y_off = [_bdot_nt(Cg[h], hin[h]) for h in heads]
    y_diag = []
    for h in heads:
        seg = jnp.where(causal, cs_col[h] - cs_t[h:h + 1, :], 0.0)
        decay = jnp.where(causal, jnp.exp(seg), 0.0)
        y_diag.append(_bdot(cb[h // (M_HEADS // M_GROUPS)] * decay, xdt[h]))
    st = [_bdot_tn(xdt[h] * jnp.exp(cs_last[h] - cs_col[h]), Bg[h]) for h in heads]
    for h in heads:
        h_scr[h] = hin[h] * jnp.exp(cs_last[h]) + st[h]
    ys = [y_diag[h] + jnp.exp(cs_col[h]) * y_off[h] + dvec_ref[:, h:h + 1] * xh[h] for h in heads]
    y = jnp.concatenate(ys, axis=-1) * _silu(z)
    gw = M_WIDTH // M_GROUPS
    parts = []
    for g in range(M_GROUPS):
        yg = y[:, g * gw:(g + 1) * gw]
        parts.append(yg * lax.rsqrt(jnp.mean(yg * yg, axis=-1, keepdims=True) + NORM_EPS))
    o_ref[0] = jnp.concatenate(parts, axis=-1) * nw_ref[...]

    @pl.when(c == nc - 1)
    def _():
        conv_out_ref[0] = prev_scr[...]
        ssm_out_ref[0] = h_scr[...]


def _ssd(proj_m, conv0_8, ssm0, wts):
    bsz, t, _ = proj_m.shape
    nc = t // CHUNK
    full = lambda shape: pl.BlockSpec(shape, lambda b, c: (0,) * len(shape))
    return pl.pallas_call(
        _ssd_kernel,
        grid=(bsz, nc),
        in_specs=[
            pl.BlockSpec((1, CHUNK, M_PROJ_PAD), lambda b, c: (b, c, 0)),
            pl.BlockSpec((1, SUBLANES, CONV_DIM), lambda b, c: (b, 0, 0)),
            pl.BlockSpec((1, M_HEADS, M_HEADDIM, M_STATE), lambda b, c: (b, 0, 0, 0)),
            full((CONV_W, CONV_DIM)), full((1, CONV_DIM)), full((1, LANES)), full((1, LANES)),
            full((1, LANES)), full((1, M_WIDTH)),
        ],
        out_specs=[
            pl.BlockSpec((1, CHUNK, M_WIDTH), lambda b, c: (b, c, 0)),
            pl.BlockSpec((1, SUBLANES, CONV_DIM), lambda b, c: (b, 0, 0)),
            pl.BlockSpec((1, M_HEADS, M_HEADDIM, M_STATE), lambda b, c: (b, 0, 0, 0)),
        ],
        out_shape=[
            jax.ShapeDtypeStruct((bsz, t, M_WIDTH), F32),
            jax.ShapeDtypeStruct((bsz, SUBLANES, CONV_DIM), F32),
            jax.ShapeDtypeStruct((bsz, M_HEADS, M_HEADDIM, M_STATE), F32),
        ],
        scratch_shapes=[pltpu.VMEM((M_HEADS, M_HEADDIM, M_STATE), F32), pltpu.VMEM((SUBLANES, CONV_DIM), F32)],
        compiler_params=pltpu.CompilerParams(
            dimension_semantics=("arbitrary", "arbitrary"), vmem_limit_bytes=40 * 1024 * 1024),
        name="ssd",
    )(proj_m, conv0_8, ssm0, *wts)


def _top16_rows(s, rowid, big):
    vals, ids = [], []
    for _ in range(P_TOPK):
        m = jnp.max(s, axis=0, keepdims=True)
        idx = jnp.min(jnp.where(s == m, rowid, big), axis=0, keepdims=True)
        s = jnp.where(rowid == idx, -jnp.inf, s)
        vals.append(m)
        ids.append(idx)
    return vals, ids


def _route_kernel(or_ref, om_ref, x_ref, wr_ref, wm_ref, n2_ref, wq_ref, keys_ref,
                  h_ref, xn_ref, off_ref, gate_ref, sv_scr, si_scr):
    tm = x_ref.shape[0]
    h = (x_ref[...]
         + jnp.dot(or_ref[...].astype(BF16), wr_ref[...], preferred_element_type=F32)
         + jnp.dot(om_ref[...].astype(BF16), wm_ref[...], preferred_element_type=F32))
    h_ref[...] = h
    xn = h * lax.rsqrt(jnp.mean(h * h, axis=-1, keepdims=True) + NORM_EPS) * n2_ref[...]
    xn_ref[...] = xn
    xn_b = xn.astype(BF16)
    rowid = lax.broadcasted_iota(jnp.int32, (N_KEYS, tm), 0)

    def sub_body(hd, carry):
        res = []
        for half in range(2):
            q = jnp.dot(xn_b, wq_ref[2 * hd + half], preferred_element_type=F32)
            s_t = _bdot_nt(keys_ref[half], q)
            res.append(_top16_rows(s_t, rowid, N_KEYS))
        for half, (vals, ids) in enumerate(res):
            sv_scr[2 * hd + half] = jnp.concatenate(vals, axis=0)
            si_scr[2 * hd + half] = jnp.concatenate(ids, axis=0)
        return carry

    lax.fori_loop(0, P_HEADS, sub_body, 0)

    r16 = lax.broadcasted_iota(jnp.int32, (16, 1), 0)
    r8 = lax.broadcasted_iota(jnp.int32, (8, 1), 0)

    def head_body(hd, carry):
        sv0, sv1 = sv_scr[2 * hd], sv_scr[2 * hd + 1]
        si0, si1 = si_scr[2 * hd], si_scr[2 * hd + 1]
        cand, cid, flat = [], [], []
        for i in range(8):
            n_i = P_TOPK // (i + 1)
            rows = 16 if i == 0 else 8
            rr = r16 if i == 0 else r8
            cand.append(jnp.where(rr < n_i, sv0[i:i + 1] + sv1[0:rows], -jnp.inf))
            cid.append(si0[i:i + 1] * N_KEYS + si1[0:rows])
            flat.append(jnp.broadcast_to(i * P_TOPK + rr, (rows, tm)))
        cand.append(sv0[8:16] + sv1[0:1])
        cid.append(si0[8:16] * N_KEYS + si1[0:1])
        flat.append(jnp.broadcast_to((8 + r8) * P_TOPK, (8, tm)))
        cand = jnp.concatenate(cand, axis=0)
        cid = jnp.concatenate(cid, axis=0)
        flat = jnp.concatenate(flat, axis=0)
        tops, eids = [], []
        for _ in range(P_TOPK):
            m = jnp.max(cand, axis=0, keepdims=True)
            sel = jnp.min(jnp.where(cand == m, flat, 4 * P_TOPK * P_TOPK), axis=0, keepdims=True)
            hit = flat == sel
            eids.append(jnp.sum(jnp.where(hit, cid, 0), axis=0, keepdims=True))
            cand = jnp.where(hit, -jnp.inf, cand)
            tops.append(m)
        top_s = jnp.concatenate(tops, axis=0)
        eid = jnp.concatenate(eids, axis=0)
        ex = jnp.exp(top_s - top_s[0:1])
        gate = ex / jnp.sum(ex, axis=0, keepdims=True)
        si_scr[hd] = TAB_PAD + TAB_ROWS_PER_EXPERT * eid
        sv_scr[hd] = gate
        return carry

    lax.fori_loop(0, P_HEADS, head_body, 0)
    off_ref[...] = jnp.concatenate([si_scr[hd] for hd in range(P_HEADS)], axis=0).T
    gate_ref[...] = jnp.concatenate([sv_scr[hd] for hd in range(P_HEADS)], axis=0).T


def _route(o_r, o_m, x2d, w_out_r, w_out_m, norm2_w, wq3, keys, tm=512):
    n = x2d.shape[0]
    tm = math.gcd(tm, n)
    full = lambda shape: pl.BlockSpec(shape, lambda i: (0,) * len(shape))
    return pl.pallas_call(
        _route_kernel,
        grid=(n // tm,),
        in_specs=[
            pl.BlockSpec((tm, R_WIDTH), lambda i: (i, 0)),
            pl.BlockSpec((tm, M_WIDTH), lambda i: (i, 0)),
            pl.BlockSpec((tm, D_MODEL), lambda i: (i, 0)),
            full((R_WIDTH, D_MODEL)), full((M_WIDTH, D_MODEL)), full((1, D_MODEL)),
            full((2 * P_HEADS, D_MODEL, P_QDIM // 2)), full((2, N_KEYS, P_QDIM // 2)),
        ],
        out_specs=[
            pl.BlockSpec((tm, D_MODEL), lambda i: (i, 0)),
            pl.BlockSpec((tm, D_MODEL), lambda i: (i, 0)),
            pl.BlockSpec((tm, PAIRS), lambda i: (i, 0)),
            pl.BlockSpec((tm, PAIRS), lambda i: (i, 0)),
        ],
        out_shape=[
            jax.ShapeDtypeStruct((n, D_MODEL), F32),
            jax.ShapeDtypeStruct((n, D_MODEL), F32),
            jax.ShapeDtypeStruct((n, PAIRS), jnp.int32),
            jax.ShapeDtypeStruct((n, PAIRS), F32),
        ],
        scratch_shapes=[pltpu.VMEM((2 * P_HEADS, P_TOPK, tm), F32), pltpu.VMEM((2 * P_HEADS, P_TOPK, tm), jnp.int32)],
        compiler_params=pltpu.CompilerParams(
            dimension_semantics=("arbitrary",), vmem_limit_bytes=48 * 1024 * 1024),
        name="route",
    )(o_r, o_m, x2d, w_out_r, w_out_m, norm2_w, wq3, keys)


def _pack_table(tab):
    tb = tab.astype(BF16)
    half = D_MODEL // 2
    lo = lax.bitcast_convert_type(tb[:, :half], jnp.uint16).astype(jnp.uint32)
    hi = lax.bitcast_convert_type(tb[:, half:], jnp.uint16).astype(jnp.uint32)
    w = (lo | (hi << 16)).reshape(N_EXPERTS * TAB_ROWS_PER_EXPERT, LANES)
    return jnp.pad(w, ((TAB_PAD, TAB_PAD), (0, 0)))


V_ROWS = 2 * SUBLANES
PEER_TOKENS_PER_STEP = 16
V_KCHUNK = 256
HID_PAD = PEER_TOKENS_PER_STEP
assert HID_PAD % SUBLANES == 0


def _merged_words(tab_vmem, offs, p_lo, p_hi, low):
    wa = tab_vmem[pl.ds(offs[p_lo], SUBLANES), :]
    wb = tab_vmem[pl.ds(offs[p_hi] - TAB_ROWS_PER_EXPERT, SUBLANES), :]
    return jnp.where(low, wa, wb)


def _peer_u_kernel(off_ref, x3_ref, gate_ref, tab_vmem, c_ref, r_scr, hid_scr):
    tb = x3_ref.shape[0]
    nsteps = tb // PEER_TOKENS_PER_STEP
    row = lax.broadcasted_iota(jnp.int32, (SUBLANES, LANES), 0)
    low = row < 4
    m2 = (row % 4) < 2
    m1 = (row % 2) == 0
    ones_b = jnp.ones((SUBLANES, LANES), BF16)

    def fold(a, b, sh, m):
        return jnp.where(m, a + pltpu.roll(a, SUBLANES - sh, 0), b + pltpu.roll(b, sh, 0))

    def flush_lane_sums(step):
        for j in range(PEER_TOKENS_PER_STEP):
            r = r_scr[j]
            hi = r.astype(BF16)
            lo = (r - hi.astype(F32)).astype(BF16)
            sums = (lax.dot_general(ones_b, hi, (((1,), (1,)), ((), ())), preferred_element_type=F32)
                    + lax.dot_general(ones_b, lo, (((1,), (1,)), ((), ())), preferred_element_type=F32))
            hid_scr[pl.ds(step * PEER_TOKENS_PER_STEP + j + HID_PAD - PEER_TOKENS_PER_STEP, 1), :] = sums[0:1, :]

    r_scr[...] = jnp.zeros(r_scr.shape, F32)

    def group_body(tg, carry):
        flush_lane_sums(tg)
        toks = [tg * PEER_TOKENS_PER_STEP + j for j in range(PEER_TOKENS_PER_STEP)]
        offs = [off_ref.at[t] for t in toks]
        xa, xb = [], []
        for t in toks:
            x = x3_ref[t]
            xr = pltpu.roll(x, 4, 0)
            xa.append(jnp.where(low, x, xr))
            xb.append(jnp.where(low, xr, x))
        for g in range(PAIRS // 8):
            pidx = [g * 8 + i for i in range(8)]
            for j in range(PEER_TOKENS_PER_STEP):
                def prod(i_lo, i_hi):
                    w = _merged_words(tab_vmem, offs[j], pidx[i_lo], pidx[i_hi], low)
                    lo = pltpu.bitcast(w << 16, F32)
                    hi = pltpu.bitcast(w & jnp.uint32(0xFFFF0000), F32)
                    return lo * xa[j] + hi * xb[j]
                n1 = fold(prod(0, 4), prod(2, 6), 2, m2)
                n2 = fold(prod(1, 5), prod(3, 7), 2, m2)
                r_scr[j, pl.ds(g * 8, 8), :] = fold(n1, n2, 1, m1)
        return carry

    lax.fori_loop(0, nsteps, group_body, 0)
    flush_lane_sums(nsteps)
    hid = hid_scr[pl.ds(HID_PAD, tb), :]
    gelu = 0.5 * hid * (1.0 + lax.erf(hid * (1.0 / math.sqrt(2.0))))
    c_ref[...] = gelu * gate_ref[...]


def _peer_u(off, x3, gate, tab, tb=128):
    n = x3.shape[0]
    tb = math.gcd(tb, n)
    return pl.pallas_call(
        _peer_u_kernel,
        grid=(n // tb,),
        in_specs=[
            pl.BlockSpec((tb, PAIRS), lambda i: (i, 0), memory_space=pltpu.SMEM),
            pl.BlockSpec((tb, SUBLANES, LANES), lambda i: (i, 0, 0)),
            pl.BlockSpec((tb, PAIRS), lambda i: (i, 0)),
            pl.BlockSpec(memory_space=pltpu.VMEM),
        ],
        out_specs=pl.BlockSpec((tb, PAIRS), lambda i: (i, 0)),
        out_shape=jax.ShapeDtypeStruct((n, PAIRS), F32),
        scratch_shapes=[pltpu.VMEM((PEER_TOKENS_PER_STEP, PAIRS, LANES), F32),
                        pltpu.VMEM((tb + HID_PAD, PAIRS), F32)],
        compiler_params=pltpu.CompilerParams(
            dimension_semantics=("arbitrary",), vmem_limit_bytes=56 * 1024 * 1024),
        name="peer_u",
    )(off, x3, gate, tab)


def _peer_v_kernel(off_ref, c_ref, h3_ref, fw_ref, tab_vmem, y_ref, crep_scr):
    tb = h3_ref.shape[0]
    kw = PAIRS * SUBLANES
    fw = fw_ref[...]
    low = lax.broadcasted_iota(jnp.int32, (SUBLANES, LANES), 0) < 4
    rp = lax.broadcasted_iota(jnp.int32, (PAIRS, kw), 0)
    rc = lax.broadcasted_iota(jnp.int32, (PAIRS, kw), 1)
    crep_scr[...] = _dot_f32_lhs(c_ref[...], (rc // SUBLANES == rp).astype(BF16))
    m = lax.broadcasted_iota(jnp.int32, (SUBLANES, kw), 0)
    col = lax.broadcasted_iota(jnp.int32, (SUBLANES, kw), 1)
    sel = m == (col % 2) * 4 + ((col % V_ROWS) // 2) % 4

    def group_body(tg, carry):
        toks = [tg * PEER_TOKENS_PER_STEP + j for j in range(PEER_TOKENS_PER_STEP)]
        offs = [off_ref.at[t] for t in toks]
        x16 = []
        for t in toks:
            lhs = jnp.where(sel, crep_scr[pl.ds(t, 1), :], 0.0)
            hi = pltpu.bitcast(pltpu.bitcast(lhs, jnp.uint32) & jnp.uint32(0xFFFF0000), F32)
            x16.append(jnp.concatenate([hi, lhs - hi], axis=0).astype(BF16))
        acc = [None] * len(toks)
        per_chunk = V_KCHUNK // V_ROWS
        for kc in range(kw // V_KCHUNK):
            for j in range(len(toks)):
                tiles = [pltpu.bitcast(_merged_words(tab_vmem, offs[j], 2 * mm, 2 * mm + 1, low), BF16)
                         for mm in range(kc * per_chunk, (kc + 1) * per_chunk)]
                part = jnp.dot(x16[j][:, kc * V_KCHUNK:(kc + 1) * V_KCHUNK], jnp.concatenate(tiles, axis=0),
                               preferred_element_type=F32)
                acc[j] = part if kc == 0 else acc[j] + part
        for j, t in enumerate(toks):
            y_ref[t] = acc[j][0:SUBLANES] + acc[j][SUBLANES:]
        return carry

    lax.fori_loop(0, tb // PEER_TOKENS_PER_STEP, group_body, 0)
    hn = h3_ref[...] + y_ref[...]
    ms = jnp.sum(jnp.sum(hn * hn, axis=2, keepdims=True), axis=1, keepdims=True) * (1.0 / D_MODEL)
    y_ref[...] = hn * lax.rsqrt(ms + NORM_EPS) * fw


def _peer_v(off, c, h3, fw, tab, tb=128):
    n = h3.shape[0]
    tb = math.gcd(tb, n)
    return pl.pallas_call(
        _peer_v_kernel,
        grid=(n // tb,),
        in_specs=[
            pl.BlockSpec((tb, PAIRS), lambda i: (i, 0), memory_space=pltpu.SMEM),
            pl.BlockSpec((tb, PAIRS), lambda i: (i, 0)),
            pl.BlockSpec((tb, SUBLANES, LANES), lambda i: (i, 0, 0)),
            pl.BlockSpec((SUBLANES, LANES), lambda i: (0, 0)),
            pl.BlockSpec(memory_space=pltpu.VMEM),
        ],
        out_specs=pl.BlockSpec((tb, SUBLANES, LANES), lambda i: (i, 0, 0)),
        out_shape=jax.ShapeDtypeStruct((n, SUBLANES, LANES), F32),
        scratch_shapes=[pltpu.VMEM((tb, PAIRS * SUBLANES), F32)],
        compiler_params=pltpu.CompilerParams(
            dimension_semantics=("arbitrary",), vmem_limit_bytes=56 * 1024 * 1024),
        name="peer_v",
    )(off, c, h3, fw, tab)


def _prep_weights(norm1_w, w_in, rwkv_mu, rwkv_w0, rwkv_w2, rwkv_a0, rwkv_a2, rwkv_g2, rwkv_k_k, rwkv_k_a,
                  rwkv_r_k, rwkv_ln_w, rwkv_ln_b, mamba_conv_w, mamba_conv_b, mamba_dt_bias, mamba_A_log,
                  mamba_D, mamba_norm_w, w_out, norm2_w, peer_w_q, peer_sub_keys, peer_u, peer_v, final_norm_w):
    row = lambda a: a.reshape(1, -1)
    pad_lanes = lambda a: jnp.pad(row(a), ((0, 0), (0, LANES - a.shape[-1])))
    w_cat = jnp.pad(w_in, ((0, 0), (0, M_PROJ_PAD - M_PROJ))).astype(BF16)
    zeros = jnp.zeros((DECAY_LORA, R_WIDTH), F32)
    rw = (row(rwkv_mu), row(rwkv_w0), jnp.concatenate([rwkv_w2, zeros], 0), row(rwkv_a0),
          jnp.concatenate([zeros, rwkv_a2], 0), rwkv_g2, row(rwkv_k_k), row(rwkv_k_a), row(rwkv_r_k),
          row(rwkv_ln_w), row(rwkv_ln_b))
    mw = (mamba_conv_w, row(mamba_conv_b), pad_lanes(mamba_dt_bias), pad_lanes(mamba_A_log),
          pad_lanes(mamba_D), row(mamba_norm_w))
    wq3 = peer_w_q.reshape(D_MODEL, 2 * P_HEADS, P_QDIM // 2).transpose(1, 0, 2).astype(BF16)
    route_w = (w_out[:R_WIDTH].astype(BF16), w_out[R_WIDTH:].astype(BF16), row(norm2_w), wq3, peer_sub_keys)
    return dict(norm1=row(norm1_w), w_cat=w_cat, rw=rw, mw=mw, route_w=route_w,
                tab_u=_pack_table(peer_u), tab_v=_pack_table(peer_v),
                fw=final_norm_w.reshape(SUBLANES, LANES))


def _trunk(x, shift0, wkv0, conv0, ssm0, w):
    bsz, t, _ = x.shape
    n = bsz * t
    x2d = x.reshape(n, D_MODEL)
    proj_r, proj_m = _in_proj(x2d, w["norm1"], w["w_cat"])
    proj_r = proj_r.reshape(bsz, t, R_PROJ)
    proj_m = proj_m.reshape(bsz, t, M_PROJ_PAD)
    o_r, new_wkv = _rwkv(proj_r, shift0, wkv0, w["rw"])
    conv0_8 = jnp.pad(conv0, ((0, 0), (SUBLANES - (CONV_W - 1), 0), (0, 0)))
    o_m, new_conv8, new_ssm = _ssd(proj_m, conv0_8, ssm0, w["mw"])
    h, xn, off, gate = _route(o_r.reshape(n, R_WIDTH), o_m.reshape(n, M_WIDTH), x2d, *w["route_w"])
    c = _peer_u(off, xn.reshape(n, SUBLANES, LANES), gate, w["tab_u"])
    y = _peer_v(off, c, h.reshape(n, SUBLANES, LANES), w["fw"], w["tab_v"])
    new_shift = proj_r[:, -1, :]
    new_conv = new_conv8[:, SUBLANES - (CONV_W - 1):, :]
    return y.reshape(bsz, t, D_MODEL), new_shift[None], new_wkv[None], new_conv[None], new_ssm[None]


def kernel(x_prompt, x_sample, state_rwkv_shift, state_rwkv_wkv, state_mamba_conv, state_mamba_ssm, norm1_w, w_in, rwkv_mu, rwkv_w0, rwkv_w2, rwkv_a0, rwkv_a2, rwkv_g2, rwkv_k_k, rwkv_k_a, rwkv_r_k, rwkv_ln_w, rwkv_ln_b, mamba_conv_w, mamba_conv_b, mamba_dt_bias, mamba_A_log, mamba_D, mamba_norm_w, w_out, norm2_w, peer_w_q, peer_sub_keys, peer_u, peer_v, final_norm_w):
    w = _prep_weights(norm1_w[0], w_in[0], rwkv_mu[0], rwkv_w0[0], rwkv_w2[0], rwkv_a0[0], rwkv_a2[0], rwkv_g2[0],
                      rwkv_k_k[0], rwkv_k_a[0], rwkv_r_k[0].reshape(-1), rwkv_ln_w[0], rwkv_ln_b[0],
                      mamba_conv_w[0], mamba_conv_b[0], mamba_dt_bias[0], mamba_A_log[0], mamba_D[0],
                      mamba_norm_w[0], w_out[0], norm2_w[0], peer_w_q[0], peer_sub_keys[0], peer_u[0], peer_v[0],
                      final_norm_w)
    bp = x_prompt.shape[0]
    zp = lambda *s: jnp.zeros((bp,) + s, F32)
    yp, *sp = _trunk(x_prompt, zp(R_PROJ), zp(R_HEADS, R_HEAD, R_HEAD), zp(CONV_W - 1, CONV_DIM),
                     zp(M_HEADS, M_HEADDIM, M_STATE), w)
    ys, *ss = _trunk(x_sample, state_rwkv_shift[0], state_rwkv_wkv[0], state_mamba_conv[0], state_mamba_ssm[0], w)
    return (yp, ys, *sp, *ss)
```

```python
import functools
import math

import jax
import jax.numpy as jnp
import numpy as np
from jax import lax
from jax.experimental import pallas as pl
from jax.experimental.pallas import tpu as pltpu

LANES = 128
SUBLANES = 8
VMEM_BYTES_V7X = 64 * 1024 * 1024

D_MODEL = 1024
CHUNK = 64
R_WIDTH = 512
R_HEAD = 64
R_HEADS = R_WIDTH // R_HEAD
DECAY_LORA = 64
AAA_LORA = 64
GATE_LORA = 128
R_PROJ = 3 * R_WIDTH + DECAY_LORA + AAA_LORA + GATE_LORA
R_GN_EPS = R_HEAD * 1e-5
M_WIDTH = 512
M_HEADDIM = 64
M_HEADS = M_WIDTH // M_HEADDIM
M_GROUPS = 2
M_STATE = 64
CONV_W = 4
CONV_DIM = M_WIDTH + 2 * M_GROUPS * M_STATE
M_PROJ = M_WIDTH + CONV_DIM + M_HEADS
M_PROJ_PAD = M_WIDTH + CONV_DIM + LANES
N_KEYS = 128
N_EXPERTS = N_KEYS * N_KEYS
P_HEADS = 8
P_QDIM = 256
P_TOPK = 16
PAIRS = P_HEADS * P_TOPK
NORM_EPS = 1e-6

HI = lax.Precision.HIGHEST
F32 = jnp.float32
BF16 = jnp.bfloat16

TAB_ROWS_PER_EXPERT = 4
TAB_PAD = 8
TAB_ROWS = N_EXPERTS * TAB_ROWS_PER_EXPERT + 2 * TAB_PAD


def _dot(a, b, precision=HI):
    return jnp.dot(a, b, precision=precision, preferred_element_type=F32)


def _dot_nt(a, b, precision=HI):
    return lax.dot_general(a, b, (((1,), (1,)), ((), ())), precision=precision, preferred_element_type=F32)


def _dot_tn(a, b, precision=HI):
    return lax.dot_general(a, b, (((0,), (0,)), ((), ())), precision=precision, preferred_element_type=F32)


def _softplus(x):
    return jnp.maximum(x, 0.0) + jnp.log(1.0 + jnp.exp(-jnp.abs(x)))


def _sigmoid(x):
    return 1.0 / (1.0 + jnp.exp(-x))


def _silu(x):
    return x * _sigmoid(x)


def _in_proj_kernel(x_ref, nw_ref, w_ref, pr_ref, pm_ref):
    x = x_ref[...]
    xn = x * lax.rsqrt(jnp.mean(x * x, axis=-1, keepdims=True) + NORM_EPS) * nw_ref[...]
    p = jnp.dot(xn.astype(BF16), w_ref[...], preferred_element_type=F32)
    pr_ref[...] = p[:, :R_PROJ]
    pm_ref[...] = p[:, R_PROJ:]


def _in_proj(x2d, norm_w, w_cat_bf16, tm=512):
    n = x2d.shape[0]
    tm = math.gcd(tm, n)
    wtot = R_PROJ + M_PROJ_PAD
    return pl.pallas_call(
        _in_proj_kernel,
        grid=(n // tm,),
        in_specs=[
            pl.BlockSpec((tm, D_MODEL), lambda i: (i, 0)),
            pl.BlockSpec((1, D_MODEL), lambda i: (0, 0)),
            pl.BlockSpec((D_MODEL, wtot), lambda i: (0, 0)),
        ],
        out_specs=[
            pl.BlockSpec((tm, R_PROJ), lambda i: (i, 0)),
            pl.BlockSpec((tm, M_PROJ_PAD), lambda i: (i, 0)),
        ],
        out_shape=[
            jax.ShapeDtypeStruct((n, R_PROJ), F32),
            jax.ShapeDtypeStruct((n, M_PROJ_PAD), F32),
        ],
        compiler_params=pltpu.CompilerParams(
            dimension_semantics=("arbitrary",), vmem_limit_bytes=48 * 1024 * 1024),
        name="in_proj",
    )(x2d, norm_w, w_cat_bf16)


def _split3(x):
    hi = x.astype(BF16)
    r1 = x - hi.astype(F32)
    mid = r1.astype(BF16)
    lo = (r1 - mid.astype(F32)).astype(BF16)
    return hi, mid, lo


def _dot_f32_lhs(x, b_exact):
    return sum(jnp.dot(piece, b_exact, preferred_element_type=F32) for piece in _split3(x))


def _dot_f32_rhs(a_exact, x):
    return sum(jnp.dot(a_exact, piece, preferred_element_type=F32) for piece in _split3(x))


def _dot_x3(a, b):
    a_hi = a.astype(BF16)
    a_lo = (a - a_hi.astype(F32)).astype(BF16)
    b_hi = b.astype(BF16)
    b_lo = (b - b_hi.astype(F32)).astype(BF16)
    d = lambda x, y: jnp.dot(x, y, preferred_element_type=F32)
    return d(a_hi, b_hi) + d(a_hi, b_lo) + d(a_lo, b_hi)


def _bdot(a, b):
    return jnp.dot(a.astype(BF16), b.astype(BF16), preferred_element_type=F32)


def _bdot_nt(a, b):
    return lax.dot_general(a.astype(BF16), b.astype(BF16), (((1,), (1,)), ((), ())), preferred_element_type=F32)


def _bdot_tn(a, b):
    return lax.dot_general(a.astype(BF16), b.astype(BF16), (((0,), (0,)), ((), ())), preferred_element_type=F32)


RWKV_ROWS_PER_STEP = 2


def _rwkv_kernel(p_ref, shift0_ref, wkv0_ref, mu_ref, w0_ref, w2_ref, a0_ref, a2_ref, g2_ref, kk_ref, ka_ref,
                 rk_ref, lnw_ref, lnb_ref, o_ref, wkv_out_ref, s_scr, prev_scr):
    c = pl.program_id(1)
    nc = pl.num_programs(1)
    L = CHUNK
    npairs = R_WIDTH // LANES
    nrows = p_ref.shape[0]

    @pl.when(c == 0)
    def _():
        for b in range(nrows):
            prev_scr[b] = jnp.broadcast_to(shift0_ref[b], prev_scr.shape[1:])
        s_scr[...] = wkv0_ref[...]

    li = lax.broadcasted_iota(jnp.int32, (LANES, LANES), 0)
    lj = lax.broadcasted_iota(jnp.int32, (LANES, LANES), 1)
    bd_mask = (li // R_HEAD) == (lj // R_HEAD)
    seg = bd_mask.astype(BF16)

    def head_sums(x):
        return jnp.concatenate([_dot_f32_lhs(x[:, LANES * j:LANES * (j + 1)], seg) for j in range(npairs)], axis=-1)

    ti = lax.broadcasted_iota(jnp.int32, (L, L), 0)
    si = lax.broadcasted_iota(jnp.int32, (L, L), 1)
    tri = (si <= ti).astype(BF16)
    row = lax.broadcasted_iota(jnp.int32, (L, 1), 0)

    rows = []
    for b in range(nrows):
        p = p_ref[b]
        p_prev = jnp.where(row == 0, prev_scr[b, 0:1, :], pltpu.roll(p, 1, 0))
        prev_scr[b] = jnp.broadcast_to(p[L - 1:L, :], prev_scr.shape[1:])
        pm = p + (p_prev - p) * mu_ref[...]
        r = pm[:, 0:R_WIDTH]
        k = pm[:, R_WIDTH:2 * R_WIDTH]
        v = pm[:, 2 * R_WIDTH:3 * R_WIDTH]
        wa = pm[:, 3 * R_WIDTH:3 * R_WIDTH + LANES]
        gd = pm[:, 3 * R_WIDTH + LANES:3 * R_WIDTH + 2 * LANES]
        w_log = -_softplus(-(w0_ref[...] + _dot_x3(jnp.tanh(wa), w2_ref[...]))) - 0.5
        lw = -jnp.exp(w_log)
        a = _sigmoid(a0_ref[...] + _dot_x3(wa, a2_ref[...]))
        g = _dot_x3(_sigmoid(gd), g2_ref[...])
        kk = k * kk_ref[...]
        kk = kk / jnp.maximum(jnp.sqrt(head_sums(kk * kk)), 1e-12)
        k_mod = k * (1.0 + (a - 1.0) * ka_ref[...])
        cl = _dot_f32_rhs(tri, lw)
        e_in = jnp.exp(cl)
        e_neg = jnp.exp(-cl)
        rows.append(dict(At=-kk * jnp.exp(cl - lw), Bt=kk * a * e_neg, Kt=k_mod * e_neg, Rt=r * e_in, v=v,
                         p_last=e_in[L - 1:L, :], bonus=head_sums(r * k_mod * rk_ref[...]) * v, g=g))

    lane = lax.broadcasted_iota(jnp.int32, (L, LANES), 1)
    trow = lax.broadcasted_iota(jnp.int32, (L, LANES), 0)
    first = lane < R_HEAD
    strict = (lane % R_HEAD) < trow
    incl = (lane % R_HEAD) <= trow
    zeros = jnp.zeros((L, LANES), F32)

    units = [(b, j) for b in range(nrows) for j in range(npairs)]
    tile = lambda b, j, name: rows[b][name][:, LANES * j:LANES * (j + 1)]
    S = [s_scr[b, j] for b, j in units]
    bk = [jnp.concatenate([tile(b, j, "Bt"), tile(b, j, "Kt")], axis=0).astype(BF16) for b, j in units]
    G = []
    for u, (b, j) in enumerate(units):
        A_, R_ = tile(b, j, "At"), tile(b, j, "Rt")
        x4 = jnp.concatenate([jnp.where(first, A_, 0.0), jnp.where(first, 0.0, A_),
                              jnp.where(first, R_, 0.0), jnp.where(first, 0.0, R_)], axis=0)
        G.append(_bdot_nt(x4, bk[u]))
    ss = [_bdot_nt(jnp.concatenate([tile(b, j, "At"), tile(b, j, "Rt")], axis=0), S[u])
          for u, (b, j) in enumerate(units)]
    m0 = [jnp.where(strict, g_[0:L], 0.0) for g_ in G]
    m1 = [jnp.where(strict, g_[L:2 * L], 0.0) for g_ in G]
    x = []
    for u, (b, j) in enumerate(units):
        zv = jnp.concatenate([zeros, tile(b, j, "v")], axis=0).astype(BF16)
        x.append(ss[u][0:L] + jnp.where(first, _bdot(m0[u], zv), _bdot(m1[u], zv)))
    n0 = [m[:, 0:R_HEAD].astype(BF16) for m in m0]
    n1 = [m[:, 0:R_HEAD].astype(BF16) for m in m1]
    for i in range(6):
        for u in range(len(units)):
            xb = x[u].astype(BF16)
            x[u] = x[u] + jnp.where(first, _bdot(n0[u], xb), _bdot(n1[u], xb))
        if i < 5:
            n0 = [_bdot(n, n).astype(BF16) for n in n0]
            n1 = [_bdot(n, n).astype(BF16) for n in n1]
    outs = [[None] * npairs for _ in range(nrows)]
    for u, (b, j) in enumerate(units):
        uv = jnp.concatenate([x[u], tile(b, j, "v")], axis=0).astype(BF16)
        mo0 = jnp.where(incl, G[u][2 * L:3 * L], 0.0)
        mo1 = jnp.where(incl, G[u][3 * L:4 * L], 0.0)
        outs[b][j] = ss[u][L:2 * L] + jnp.where(first, _bdot(mo0, uv), _bdot(mo1, uv))
        S[u] = (S[u] + jnp.where(bd_mask, _bdot_tn(uv, bk[u]), 0.0)) * tile(b, j, "p_last")
    for u, (b, j) in enumerate(units):
        s_scr[b, j] = S[u]
    for b in range(nrows):
        o = jnp.concatenate(outs[b], axis=-1)
        mean = head_sums(o) * (1.0 / R_HEAD)
        d = o - mean
        var = head_sums(d * d) * (1.0 / R_HEAD)
        o = d * lax.rsqrt(var + R_GN_EPS)
        o_ref[b] = (o * lnw_ref[...] + lnb_ref[...] + rows[b]["bonus"]) * rows[b]["g"]

    @pl.when(c == nc - 1)
    def _():
        wkv_out_ref[...] = s_scr[...]


def _rwkv(proj_r, shift0, wkv0, wts):
    bsz, t, _ = proj_r.shape
    nc = t // CHUNK
    npairs = R_WIDTH // LANES
    w5 = wkv0.reshape(bsz, npairs, 2, R_HEAD, R_HEAD)
    zero = jnp.zeros_like(w5[:, :, 0])
    wkv_bd = jnp.concatenate([jnp.concatenate([w5[:, :, 0], zero], axis=-1),
                              jnp.concatenate([zero, w5[:, :, 1]], axis=-1)], axis=-2)
    full = lambda shape: pl.BlockSpec(shape, lambda b, c: (0,) * len(shape))
    nb = math.gcd(RWKV_ROWS_PER_STEP, bsz)
    o_r, s_bd = pl.pallas_call(
        _rwkv_kernel,
        grid=(bsz // nb, nc),
        in_specs=[
            pl.BlockSpec((nb, CHUNK, R_PROJ), lambda b, c: (b, c, 0)),
            pl.BlockSpec((nb, 1, R_PROJ), lambda b, c: (b, 0, 0)),
            pl.BlockSpec((nb, npairs, LANES, LANES), lambda b, c: (b, 0, 0, 0)),
            full((1, R_PROJ)), full((1, R_WIDTH)), full((LANES, R_WIDTH)), full((1, R_WIDTH)),
            full((LANES, R_WIDTH)), full((GATE_LORA, R_WIDTH)), full((1, R_WIDTH)), full((1, R_WIDTH)),
            full((1, R_WIDTH)), full((1, R_WIDTH)), full((1, R_WIDTH)),
        ],
        out_specs=[
            pl.BlockSpec((nb, CHUNK, R_WIDTH), lambda b, c: (b, c, 0)),
            pl.BlockSpec((nb, npairs, LANES, LANES), lambda b, c: (b, 0, 0, 0)),
        ],
        out_shape=[
            jax.ShapeDtypeStruct((bsz, t, R_WIDTH), F32),
            jax.ShapeDtypeStruct((bsz, npairs, LANES, LANES), F32),
        ],
        scratch_shapes=[pltpu.VMEM((nb, npairs, LANES, LANES), F32), pltpu.VMEM((nb, SUBLANES, R_PROJ), F32)],
        compiler_params=pltpu.CompilerParams(
            dimension_semantics=("arbitrary", "arbitrary"), vmem_limit_bytes=40 * 1024 * 1024),
        name="rwkv",
    )(proj_r, shift0.reshape(bsz, 1, R_PROJ), wkv_bd, *wts)
    new_wkv = jnp.stack([s_bd[:, :, :R_HEAD, :R_HEAD], s_bd[:, :, R_HEAD:, R_HEAD:]], axis=2)
    return o_r, new_wkv.reshape(bsz, R_HEADS, R_HEAD, R_HEAD)


def _ssd_kernel(p_ref, conv0_ref, ssm0_ref, cw_ref, cb_ref, dtb_ref, alog_ref, dvec_ref, nw_ref,
                o_ref, conv_out_ref, ssm_out_ref, h_scr, prev_scr):
    c = pl.program_id(1)
    nc = pl.num_programs(1)
    L = CHUNK

    @pl.when(c == 0)
    def _():
        prev_scr[...] = conv0_ref[0]
        h_scr[...] = ssm0_ref[0]

    p = p_ref[0]
    z = p[:, 0:M_WIDTH]
    xbc = p[:, M_WIDTH:M_WIDTH + CONV_DIM]
    dt_raw = p[:, M_WIDTH + CONV_DIM:]
    prev8 = prev_scr[...]
    row8 = lax.broadcasted_iota(jnp.int32, (SUBLANES, 1), 0)
    conv = xbc * cw_ref[CONV_W - 1:CONV_W, :]
    for j in range(1, CONV_W):
        xs_j = pltpu.roll(xbc, j, 0)
        first = jnp.where(row8 >= j, xs_j[0:SUBLANES], pltpu.roll(prev8, j, 0))
        shifted = jnp.concatenate([first, xs_j[SUBLANES:]], axis=0)
        conv = conv + shifted * cw_ref[CONV_W - 1 - j:CONV_W - j, :]
    prev_scr[...] = xbc[L - SUBLANES:L, :]
    act = _silu(conv + cb_ref[...])
    xs = act[:, 0:M_WIDTH]
    Bm = act[:, M_WIDTH:M_WIDTH + M_GROUPS * M_STATE]
    Cm = act[:, M_WIDTH + M_GROUPS * M_STATE:]
    dt = _softplus(dt_raw + dtb_ref[...])
    A = -jnp.exp(alog_ref[...])
    ti = lax.broadcasted_iota(jnp.int32, (L, L), 0)
    si = lax.broadcasted_iota(jnp.int32, (L, L), 1)
    causal = si <= ti
    cs = _dot_f32_rhs(causal.astype(BF16), dt * A)
    cs_t = cs.T
    heads = range(M_HEADS)
    group = lambda h: slice((h // (M_HEADS // M_GROUPS)) * M_STATE, (h // (M_HEADS // M_GROUPS) + 1) * M_STATE)
    Bg = [Bm[:, group(h)].astype(BF16) for h in heads]
    Cg = [Cm[:, group(h)].astype(BF16) for h in heads]
    cb = [_bdot_nt(Cm[:, g * M_STATE:(g + 1) * M_STATE], Bm[:, g * M_STATE:(g + 1) * M_STATE]) for g in range(M_GROUPS)]
    hin = [h_scr[h] for h in heads]
    xh = [xs[:, h * M_HEADDIM:(h + 1) * M_HEADDIM] for h in heads]
    xdt = [xh[h] * dt[:, h:h + 1] for h in heads]
    cs_col = [cs[:, h:h + 1] for h in heads]
    cs_last = [cs[L - 1:L, h:h + 1] for h in heads]
    y_off = [_bdot_nt(Cg[h], hin[h]) for h in heads]
    y_diag = []
    for h in heads:
        seg = jnp.where(causal, cs_col[h] - cs_t[h:h + 1, :], 0.0)
        decay = jnp.where(causal, jnp.exp(seg), 0.0)
        y_diag.append(_bdot(cb[h // (M_HEADS // M_GROUPS)] * decay, xdt[h]))
    st = [_bdot_tn(xdt[h] * jnp.exp(cs_last[h] - cs_col[h]), Bg[h]) for h in heads]
    for h in heads:
        h_scr[h] = hin[h] * jnp.exp(cs_last[h]) + st[h]
    ys = [y_diag[h] + jnp.exp(cs_col[h]) * y_off[h] + dvec_ref[:, h:h + 1] * xh[h] for h in heads]
    y = jnp.concatenate(ys, axis=-1) * _silu(z)
    gw = M_WIDTH // M_GROUPS
    parts = []
    for g in range(M_GROUPS):
        yg = y[:, g * gw:(g + 1) * gw]
        parts.append(yg * lax.rsqrt(jnp.mean(yg * yg, axis=-1, keepdims=True) + NORM_EPS))
    o_ref[0] = jnp.concatenate(parts, axis=-1) * nw_ref[...]

    @pl.when(c == nc - 1)
    def _():
        conv_out_ref[0] = prev_scr[...]
        ssm_out_ref[0] = h_scr[...]


def _ssd(proj_m, conv0_8, ssm0, wts):
    bsz, t, _ = proj_m.shape
    nc = t // CHUNK
    full = lambda shape: pl.BlockSpec(shape, lambda b, c: (0,) * len(shape))
    return pl.pallas_call(
        _ssd_kernel,
        grid=(bsz, nc),
        in_specs=[
            pl.BlockSpec((1, CHUNK, M_PROJ_PAD), lambda b, c: (b, c, 0)),
            pl.BlockSpec((1, SUBLANES, CONV_DIM), lambda b, c: (b, 0, 0)),
            pl.BlockSpec((1, M_HEADS, M_HEADDIM, M_STATE), lambda b, c: (b, 0, 0, 0)),
            full((CONV_W, CONV_DIM)), full((1, CONV_DIM)), full((1, LANES)), full((1, LANES)),
            full((1, LANES)), full((1, M_WIDTH)),
        ],
        out_specs=[
            pl.BlockSpec((1, CHUNK, M_WIDTH), lambda b, c: (b, c, 0)),
            pl.BlockSpec((1, SUBLANES, CONV_DIM), lambda b, c: (b, 0, 0)),
            pl.BlockSpec((1, M_HEADS, M_HEADDIM, M_STATE), lambda b, c: (b, 0, 0, 0)),
        ],
        out_shape=[
            jax.ShapeDtypeStruct((bsz, t, M_WIDTH), F32),
            jax.ShapeDtypeStruct((bsz, SUBLANES, CONV_DIM), F32),
            jax.ShapeDtypeStruct((bsz, M_HEADS, M_HEADDIM, M_STATE), F32),
        ],
        scratch_shapes=[pltpu.VMEM((M_HEADS, M_HEADDIM, M_STATE), F32), pltpu.VMEM((SUBLANES, CONV_DIM), F32)],
        compiler_params=pltpu.CompilerParams(
            dimension_semantics=("arbitrary", "arbitrary"), vmem_limit_bytes=40 * 1024 * 1024),
        name="ssd",
    )(proj_m, conv0_8, ssm0, *wts)


def _top16_rows(s, rowid, big):
    vals, ids = [], []
    for _ in range(P_TOPK):
        m = jnp.max(s, axis=0, keepdims=True)
        idx = jnp.min(jnp.where(s == m, rowid, big), axis=0, keepdims=True)
        s = jnp.where(rowid == idx, -jnp.inf, s)
        vals.append(m)
        ids.append(idx)
    return vals, ids


def _route_kernel(or_ref, om_ref, x_ref, wr_ref, wm_ref, n2_ref, wq_ref, keys_ref,
                  h_ref, xn_ref, off_ref, gate_ref, sv_scr, si_scr):
    tm = x_ref.shape[0]
    h = (x_ref[...]
         + jnp.dot(or_ref[...].astype(BF16), wr_ref[...], preferred_element_type=F32)
         + jnp.dot(om_ref[...].astype(BF16), wm_ref[...], preferred_element_type=F32))
    h_ref[...] = h
    xn = h * lax.rsqrt(jnp.mean(h * h, axis=-1, keepdims=True) + NORM_EPS) * n2_ref[...]
    xn_ref[...] = xn
    xn_b = xn.astype(BF16)
    rowid = lax.broadcasted_iota(jnp.int32, (N_KEYS, tm), 0)

    def sub_body(hd, carry):
        res = []
        for half in range(2):
            q = jnp.dot(xn_b, wq_ref[2 * hd + half], preferred_element_type=F32)
            s_t = _bdot_nt(keys_ref[half], q)
            res.append(_top16_rows(s_t, rowid, N_KEYS))
        for half, (vals, ids) in enumerate(res):
            sv_scr[2 * hd + half] = jnp.concatenate(vals, axis=0)
            si_scr[2 * hd + half] = jnp.concatenate(ids, axis=0)
        return carry

    lax.fori_loop(0, P_HEADS, sub_body, 0)

    r16 = lax.broadcasted_iota(jnp.int32, (16, 1), 0)
    r8 = lax.broadcasted_iota(jnp.int32, (8, 1), 0)

    def head_body(hd, carry):
        sv0, sv1 = sv_scr[2 * hd], sv_scr[2 * hd + 1]
        si0, si1 = si_scr[2 * hd], si_scr[2 * hd + 1]
        cand, cid, flat = [], [], []
        for i in range(8):
            n_i = P_TOPK // (i + 1)
            rows = 16 if i == 0 else 8
            rr = r16 if i == 0 else r8
            cand.append(jnp.where(rr < n_i, sv0[i:i + 1] + sv1[0:rows], -jnp.inf))
            cid.append(si0[i:i + 1] * N_KEYS + si1[0:rows])
            flat.append(jnp.broadcast_to(i * P_TOPK + rr, (rows, tm)))
        cand.append(sv0[8:16] + sv1[0:1])
        cid.append(si0[8:16] * N_KEYS + si1[0:1])
        flat.append(jnp.broadcast_to((8 + r8) * P_TOPK, (8, tm)))
        cand = jnp.concatenate(cand, axis=0)
        cid = jnp.concatenate(cid, axis=0)
        flat = jnp.concatenate(flat, axis=0)
        tops, eids = [], []
        for _ in range(P_TOPK):
            m = jnp.max(cand, axis=0, keepdims=True)
            sel = jnp.min(jnp.where(cand == m, flat, 4 * P_TOPK * P_TOPK), axis=0, keepdims=True)
            hit = flat == sel
            eids.append(jnp.sum(jnp.where(hit, cid, 0), axis=0, keepdims=True))
            cand = jnp.where(hit, -jnp.inf, cand)
            tops.append(m)
        top_s = jnp.concatenate(tops, axis=0)
        eid = jnp.concatenate(eids, axis=0)
        ex = jnp.exp(top_s - top_s[0:1])
        gate = ex / jnp.sum(ex, axis=0, keepdims=True)
        si_scr[hd] = TAB_PAD + TAB_ROWS_PER_EXPERT * eid
        sv_scr[hd] = gate
        return carry

    lax.fori_loop(0, P_HEADS, head_body, 0)
    off_ref[...] = jnp.concatenate([si_scr[hd] for hd in range(P_HEADS)], axis=0).T
    gate_ref[...] = jnp.concatenate([sv_scr[hd] for hd in range(P_HEADS)], axis=0).T


def _route(o_r, o_m, x2d, w_out_r, w_out_m, norm2_w, wq3, keys, tm=512):
    n = x2d.shape[0]
    tm = math.gcd(tm, n)
    full = lambda shape: pl.BlockSpec(shape, lambda i: (0,) * len(shape))
    return pl.pallas_call(
        _route_kernel,
        grid=(n // tm,),
        in_specs=[
            pl.BlockSpec((tm, R_WIDTH), lambda i: (i, 0)),
            pl.BlockSpec((tm, M_WIDTH), lambda i: (i, 0)),
            pl.BlockSpec((tm, D_MODEL), lambda i: (i, 0)),
            full((R_WIDTH, D_MODEL)), full((M_WIDTH, D_MODEL)), full((1, D_MODEL)),
            full((2 * P_HEADS, D_MODEL, P_QDIM // 2)), full((2, N_KEYS, P_QDIM // 2)),
        ],
        out_specs=[
            pl.BlockSpec((tm, D_MODEL), lambda i: (i, 0)),
            pl.BlockSpec((tm, D_MODEL), lambda i: (i, 0)),
            pl.BlockSpec((tm, PAIRS), lambda i: (i, 0)),
            pl.BlockSpec((tm, PAIRS), lambda i: (i, 0)),
        ],
        out_shape=[
            jax.ShapeDtypeStruct((n, D_MODEL), F32),
            jax.ShapeDtypeStruct((n, D_MODEL), F32),
            jax.ShapeDtypeStruct((n, PAIRS), jnp.int32),
            jax.ShapeDtypeStruct((n, PAIRS), F32),
        ],
        scratch_shapes=[pltpu.VMEM((2 * P_HEADS, P_TOPK, tm), F32), pltpu.VMEM((2 * P_HEADS, P_TOPK, tm), jnp.int32)],
        compiler_params=pltpu.CompilerParams(
            dimension_semantics=("arbitrary",), vmem_limit_bytes=48 * 1024 * 1024),
        name="route",
    )(o_r, o_m, x2d, w_out_r, w_out_m, norm2_w, wq3, keys)


def _pack_table(tab):
    tb = tab.astype(BF16)
    half = D_MODEL // 2
    lo = lax.bitcast_convert_type(tb[:, :half], jnp.uint16).astype(jnp.uint32)
    hi = lax.bitcast_convert_type(tb[:, half:], jnp.uint16).astype(jnp.uint32)
    w = (lo | (hi << 16)).reshape(N_EXPERTS * TAB_ROWS_PER_EXPERT, LANES)
    return jnp.pad(w, ((TAB_PAD, TAB_PAD), (0, 0)))


V_ROWS = 2 * SUBLANES
PEER_TOKENS_PER_STEP = 16
V_KCHUNK = 256
HID_PAD = PEER_TOKENS_PER_STEP
assert HID_PAD % SUBLANES == 0


IDX_SETS = 2


def _idx_copy(off_vmem, idx_smem, idx_sem, row0, s):
    return pltpu.make_async_copy(off_vmem.at[pl.ds(row0, PEER_TOKENS_PER_STEP)], idx_smem.at[s], idx_sem.at[s])


def _staged_index_loop(off_vmem, idx_smem, idx_sem, tb, step):
    rows_per_trip = IDX_SETS * PEER_TOKENS_PER_STEP
    ntrips = tb // rows_per_trip
    for s in range(IDX_SETS):
        _idx_copy(off_vmem, idx_smem, idx_sem, s * PEER_TOKENS_PER_STEP, s).start()

    def trip(k, carry):
        for s in range(IDX_SETS):
            row0 = pl.multiple_of(k * rows_per_trip + s * PEER_TOKENS_PER_STEP, PEER_TOKENS_PER_STEP)
            _idx_copy(off_vmem, idx_smem, idx_sem, row0, s).wait()
            step(row0, [idx_smem.at[s, j] for j in range(PEER_TOKENS_PER_STEP)])

            @pl.when(k + 1 < ntrips)
            def _():
                _idx_copy(off_vmem, idx_smem, idx_sem, row0 + rows_per_trip, s).start()
        return carry

    lax.fori_loop(0, ntrips, trip, 0)


def _merged_words(tab_vmem, offs, p_lo, p_hi, low):
    wa = tab_vmem[pl.ds(offs[p_lo], SUBLANES), :]
    wb = tab_vmem[pl.ds(offs[p_hi] - TAB_ROWS_PER_EXPERT, SUBLANES), :]
    return jnp.where(low, wa, wb)


def _peer_u_kernel(off_ref, x3_ref, gate_ref, tab_vmem, c_ref, r_scr, hid_scr, idx_smem, idx_sem):
    tb = x3_ref.shape[0]
    nsteps = tb // PEER_TOKENS_PER_STEP
    row = lax.broadcasted_iota(jnp.int32, (SUBLANES, LANES), 0)
    low = row < 4
    m2 = (row % 4) < 2
    m1 = (row % 2) == 0
    ones_b = jnp.ones((SUBLANES, LANES), BF16)

    def fold(a, b, sh, m):
        return jnp.where(m, a + pltpu.roll(a, SUBLANES - sh, 0), b + pltpu.roll(b, sh, 0))

    def flush_lane_sums(row0):
        for j in range(PEER_TOKENS_PER_STEP):
            r = r_scr[j]
            hi = r.astype(BF16)
            lo = (r - hi.astype(F32)).astype(BF16)
            sums = (lax.dot_general(ones_b, hi, (((1,), (1,)), ((), ())), preferred_element_type=F32)
                    + lax.dot_general(ones_b, lo, (((1,), (1,)), ((), ())), preferred_element_type=F32))
            hid_scr[pl.ds(row0 + j + HID_PAD - PEER_TOKENS_PER_STEP, 1), :] = sums[0:1, :]

    r_scr[...] = jnp.zeros(r_scr.shape, F32)

    def step(row0, offs):
        flush_lane_sums(row0)
        toks = [row0 + j for j in range(PEER_TOKENS_PER_STEP)]
        xa, xb = [], []
        for t in toks:
            x = x3_ref[t]
            xr = pltpu.roll(x, 4, 0)
            xa.append(jnp.where(low, x, xr))
            xb.append(jnp.where(low, xr, x))
        for g in range(PAIRS // 8):
            pidx = [g * 8 + i for i in range(8)]
            for j in range(PEER_TOKENS_PER_STEP):
                def prod(i_lo, i_hi):
                    w = _merged_words(tab_vmem, offs[j], pidx[i_lo], pidx[i_hi], low)
                    lo = pltpu.bitcast(w << 16, F32)
                    hi = pltpu.bitcast(w & jnp.uint32(0xFFFF0000), F32)
                    return lo * xa[j] + hi * xb[j]
                n1 = fold(prod(0, 4), prod(2, 6), 2, m2)
                n2 = fold(prod(1, 5), prod(3, 7), 2, m2)
                r_scr[j, pl.ds(g * 8, 8), :] = fold(n1, n2, 1, m1)

    _staged_index_loop(off_ref, idx_smem, idx_sem, tb, step)
    flush_lane_sums(tb)
    hid = hid_scr[pl.ds(HID_PAD, tb), :]
    gelu = 0.5 * hid * (1.0 + lax.erf(hid * (1.0 / math.sqrt(2.0))))
    c_ref[...] = gelu * gate_ref[...]


def _peer_u(off, x3, gate, tab, tb=256):
    n = x3.shape[0]
    tb = math.gcd(tb, n)
    assert tb % (IDX_SETS * PEER_TOKENS_PER_STEP) == 0
    return pl.pallas_call(
        _peer_u_kernel,
        grid=(n // tb,),
        in_specs=[
            pl.BlockSpec((tb, PAIRS), lambda i: (i, 0)),
            pl.BlockSpec((tb, SUBLANES, LANES), lambda i: (i, 0, 0)),
            pl.BlockSpec((tb, PAIRS), lambda i: (i, 0)),
            pl.BlockSpec(memory_space=pltpu.VMEM),
        ],
        out_specs=pl.BlockSpec((tb, PAIRS), lambda i: (i, 0)),
        out_shape=jax.ShapeDtypeStruct((n, PAIRS), F32),
        scratch_shapes=[pltpu.VMEM((PEER_TOKENS_PER_STEP, PAIRS, LANES), F32),
                        pltpu.VMEM((tb + HID_PAD, PAIRS), F32),
                        pltpu.SMEM((IDX_SETS, PEER_TOKENS_PER_STEP, PAIRS), jnp.int32),
                        pltpu.SemaphoreType.DMA((IDX_SETS,))],
        compiler_params=pltpu.CompilerParams(
            dimension_semantics=("arbitrary",), vmem_limit_bytes=56 * 1024 * 1024),
        name="peer_u",
    )(off, x3, gate, tab)


def _peer_v_kernel(off_ref, c_ref, h3_ref, fw_ref, tab_vmem, y_ref, crep_scr, idx_smem, idx_sem):
    tb = h3_ref.shape[0]
    kw = PAIRS * SUBLANES
    fw = fw_ref[...]
    low = lax.broadcasted_iota(jnp.int32, (SUBLANES, LANES), 0) < 4
    rp = lax.broadcasted_iota(jnp.int32, (PAIRS, kw), 0)
    rc = lax.broadcasted_iota(jnp.int32, (PAIRS, kw), 1)
    crep_scr[...] = _dot_f32_lhs(c_ref[...], (rc // SUBLANES == rp).astype(BF16))
    m = lax.broadcasted_iota(jnp.int32, (SUBLANES, kw), 0)
    col = lax.broadcasted_iota(jnp.int32, (SUBLANES, kw), 1)
    sel = m == (col % 2) * 4 + ((col % V_ROWS) // 2) % 4

    def step(row0, offs):
        toks = [row0 + j for j in range(PEER_TOKENS_PER_STEP)]
        x16 = []
        for t in toks:
            lhs = jnp.where(sel, crep_scr[pl.ds(t, 1), :], 0.0)
            hi = pltpu.bitcast(pltpu.bitcast(lhs, jnp.uint32) & jnp.uint32(0xFFFF0000), F32)
            x16.append(jnp.concatenate([hi, lhs - hi], axis=0).astype(BF16))
        acc = [None] * len(toks)
        per_chunk = V_KCHUNK // V_ROWS
        for kc in range(kw // V_KCHUNK):
            for j in range(len(toks)):
                tiles = [pltpu.bitcast(_merged_words(tab_vmem, offs[j], 2 * mm, 2 * mm + 1, low), BF16)
                         for mm in range(kc * per_chunk, (kc + 1) * per_chunk)]
                part = jnp.dot(x16[j][:, kc * V_KCHUNK:(kc + 1) * V_KCHUNK], jnp.concatenate(tiles, axis=0),
                               preferred_element_type=F32)
                acc[j] = part if kc == 0 else acc[j] + part
        for j, t in enumerate(toks):
            y_ref[t] = acc[j][0:SUBLANES] + acc[j][SUBLANES:]

    _staged_index_loop(off_ref, idx_smem, idx_sem, tb, step)
    hn = h3_ref[...] + y_ref[...]
    ms = jnp.sum(jnp.sum(hn * hn, axis=2, keepdims=True), axis=1, keepdims=True) * (1.0 / D_MODEL)
    y_ref[...] = hn * lax.rsqrt(ms + NORM_EPS) * fw


def _peer_v(off, c, h3, fw, tab, tb=256):
    n = h3.shape[0]
    tb = math.gcd(tb, n)
    assert tb % (IDX_SETS * PEER_TOKENS_PER_STEP) == 0
    return pl.pallas_call(
        _peer_v_kernel,
        grid=(n // tb,),
        in_specs=[
            pl.BlockSpec((tb, PAIRS), lambda i: (i, 0)),
            pl.BlockSpec((tb, PAIRS), lambda i: (i, 0)),
            pl.BlockSpec((tb, SUBLANES, LANES), lambda i: (i, 0, 0)),
            pl.BlockSpec((SUBLANES, LANES), lambda i: (0, 0)),
            pl.BlockSpec(memory_space=pltpu.VMEM),
        ],
        out_specs=pl.BlockSpec((tb, SUBLANES, LANES), lambda i: (i, 0, 0)),
        out_shape=jax.ShapeDtypeStruct((n, SUBLANES, LANES), F32),
        scratch_shapes=[pltpu.VMEM((tb, PAIRS * SUBLANES), F32),
                        pltpu.SMEM((IDX_SETS, PEER_TOKENS_PER_STEP, PAIRS), jnp.int32),
                        pltpu.SemaphoreType.DMA((IDX_SETS,))],
        compiler_params=pltpu.CompilerParams(
            dimension_semantics=("arbitrary",), vmem_limit_bytes=56 * 1024 * 1024),
        name="peer_v",
    )(off, c, h3, fw, tab)


def _prep_weights(norm1_w, w_in, rwkv_mu, rwkv_w0, rwkv_w2, rwkv_a0, rwkv_a2, rwkv_g2, rwkv_k_k, rwkv_k_a,
                  rwkv_r_k, rwkv_ln_w, rwkv_ln_b, mamba_conv_w, mamba_conv_b, mamba_dt_bias, mamba_A_log,
                  mamba_D, mamba_norm_w, w_out, norm2_w, peer_w_q, peer_sub_keys, peer_u, peer_v, final_norm_w):
    row = lambda a: a.reshape(1, -1)
    pad_lanes = lambda a: jnp.pad(row(a), ((0, 0), (0, LANES - a.shape[-1])))
    w_cat = jnp.pad(w_in, ((0, 0), (0, M_PROJ_PAD - M_PROJ))).astype(BF16)
    zeros = jnp.zeros((DECAY_LORA, R_WIDTH), F32)
    rw = (row(rwkv_mu), row(rwkv_w0), jnp.concatenate([rwkv_w2, zeros], 0), row(rwkv_a0),
          jnp.concatenate([zeros, rwkv_a2], 0), rwkv_g2, row(rwkv_k_k), row(rwkv_k_a), row(rwkv_r_k),
          row(rwkv_ln_w), row(rwkv_ln_b))
    mw = (mamba_conv_w, row(mamba_conv_b), pad_lanes(mamba_dt_bias), pad_lanes(mamba_A_log),
          pad_lanes(mamba_D), row(mamba_norm_w))
    wq3 = peer_w_q.reshape(D_MODEL, 2 * P_HEADS, P_QDIM // 2).transpose(1, 0, 2).astype(BF16)
    route_w = (w_out[:R_WIDTH].astype(BF16), w_out[R_WIDTH:].astype(BF16), row(norm2_w), wq3, peer_sub_keys)
    return dict(norm1=row(norm1_w), w_cat=w_cat, rw=rw, mw=mw, route_w=route_w,
                tab_u=_pack_table(peer_u), tab_v=_pack_table(peer_v),
                fw=final_norm_w.reshape(SUBLANES, LANES))


def _trunk(x, shift0, wkv0, conv0, ssm0, w):
    bsz, t, _ = x.shape
    n = bsz * t
    x2d = x.reshape(n, D_MODEL)
    proj_r, proj_m = _in_proj(x2d, w["norm1"], w["w_cat"])
    proj_r = proj_r.reshape(bsz, t, R_PROJ)
    proj_m = proj_m.reshape(bsz, t, M_PROJ_PAD)
    o_r, new_wkv = _rwkv(proj_r, shift0, wkv0, w["rw"])
    conv0_8 = jnp.pad(conv0, ((0, 0), (SUBLANES - (CONV_W - 1), 0), (0, 0)))
    o_m, new_conv8, new_ssm = _ssd(proj_m, conv0_8, ssm0, w["mw"])
    h, xn, off, gate = _route(o_r.reshape(n, R_WIDTH), o_m.reshape(n, M_WIDTH), x2d, *w["route_w"])
    c = _peer_u(off, xn.reshape(n, SUBLANES, LANES), gate, w["tab_u"])
    y = _peer_v(off, c, h.reshape(n, SUBLANES, LANES), w["fw"], w["tab_v"])
    new_shift = proj_r[:, -1, :]
    new_conv = new_conv8[:, SUBLANES - (CONV_W - 1):, :]
    return y.reshape(bsz, t, D_MODEL), new_shift[None], new_wkv[None], new_conv[None], new_ssm[None]


def kernel(x_prompt, x_sample, state_rwkv_shift, state_rwkv_wkv, state_mamba_conv, state_mamba_ssm, norm1_w, w_in, rwkv_mu, rwkv_w0, rwkv_w2, rwkv_a0, rwkv_a2, rwkv_g2, rwkv_k_k, rwkv_k_a, rwkv_r_k, rwkv_ln_w, rwkv_ln_b, mamba_conv_w, mamba_conv_b, mamba_dt_bias, mamba_A_log, mamba_D, mamba_norm_w, w_out, norm2_w, peer_w_q, peer_sub_keys, peer_u, peer_v, final_norm_w):
    w = _prep_weights(norm1_w[0], w_in[0], rwkv_mu[0], rwkv_w0[0], rwkv_w2[0], rwkv_a0[0], rwkv_a2[0], rwkv_g2[0],
                      rwkv_k_k[0], rwkv_k_a[0], rwkv_r_k[0].reshape(-1), rwkv_ln_w[0], rwkv_ln_b[0],
                      mamba_conv_w[0], mamba_conv_b[0], mamba_dt_bias[0], mamba_A_log[0], mamba_D[0],
                      mamba_norm_w[0], w_out[0], norm2_w[0], peer_w_q[0], peer_sub_keys[0], peer_u[0], peer_v[0],
                      final_norm_w)
    bp = x_prompt.shape[0]
    zp = lambda *s: jnp.zeros((bp,) + s, F32)
    yp, *sp = _trunk(x_prompt, zp(R_PROJ), zp(R_HEADS, R_HEAD, R_HEAD), zp(CONV_W - 1, CONV_DIM),
                     zp(M_HEADS, M_HEADDIM, M_STATE), w)
    ys, *ss = _trunk(x_sample, state_rwkv_shift[0], state_rwkv_wkv[0], state_mamba_conv[0], state_mamba_ssm[0], w)
    return (yp, ys, *sp, *ss)
```

```python
import functools
import math

import jax
import jax.numpy as jnp
import numpy as np
from jax import lax
from jax.experimental import pallas as pl
from jax.experimental.pallas import tpu as pltpu

LANES = 128
SUBLANES = 8
VMEM_BYTES_V7X = 64 * 1024 * 1024

D_MODEL = 1024
CHUNK = 64
R_WIDTH = 512
R_HEAD = 64
R_HEADS = R_WIDTH // R_HEAD
DECAY_LORA = 64
AAA_LORA = 64
GATE_LORA = 128
R_PROJ = 3 * R_WIDTH + DECAY_LORA + AAA_LORA + GATE_LORA
R_GN_EPS = R_HEAD * 1e-5
M_WIDTH = 512
M_HEADDIM = 64
M_HEADS = M_WIDTH // M_HEADDIM
M_GROUPS = 2
M_STATE = 64
CONV_W = 4
CONV_DIM = M_WIDTH + 2 * M_GROUPS * M_STATE
M_PROJ = M_WIDTH + CONV_DIM + M_HEADS
M_PROJ_PAD = M_WIDTH + CONV_DIM + LANES
N_KEYS = 128
N_EXPERTS = N_KEYS * N_KEYS
P_HEADS = 8
P_QDIM = 256
P_TOPK = 16
PAIRS = P_HEADS * P_TOPK
NORM_EPS = 1e-6

HI = lax.Precision.HIGHEST
F32 = jnp.float32
BF16 = jnp.bfloat16

TAB_ROWS_PER_EXPERT = 4
TAB_PAD = 8
TAB_ROWS = N_EXPERTS * TAB_ROWS_PER_EXPERT + 2 * TAB_PAD


def _dot(a, b, precision=HI):
    return jnp.dot(a, b, precision=precision, preferred_element_type=F32)


def _dot_nt(a, b, precision=HI):
    return lax.dot_general(a, b, (((1,), (1,)), ((), ())), precision=precision, preferred_element_type=F32)


def _dot_tn(a, b, precision=HI):
    return lax.dot_general(a, b, (((0,), (0,)), ((), ())), precision=precision, preferred_element_type=F32)


def _softplus(x):
    return jnp.maximum(x, 0.0) + jnp.log(1.0 + jnp.exp(-jnp.abs(x)))


def _sigmoid(x):
    return 1.0 / (1.0 + jnp.exp(-x))


def _silu(x):
    return x * _sigmoid(x)


def _in_proj_kernel(x_ref, nw_ref, w_ref, pr_ref, pm_ref):
    x = x_ref[...]
    xn = x * lax.rsqrt(jnp.mean(x * x, axis=-1, keepdims=True) + NORM_EPS) * nw_ref[...]
    p = jnp.dot(xn.astype(BF16), w_ref[...], preferred_element_type=F32)
    pr_ref[...] = p[:, :R_PROJ]
    pm_ref[...] = p[:, R_PROJ:]


def _in_proj(x2d, norm_w, w_cat_bf16, tm=512):
    n = x2d.shape[0]
    tm = math.gcd(tm, n)
    wtot = R_PROJ + M_PROJ_PAD
    return pl.pallas_call(
        _in_proj_kernel,
        grid=(n // tm,),
        in_specs=[
            pl.BlockSpec((tm, D_MODEL), lambda i: (i, 0)),
            pl.BlockSpec((1, D_MODEL), lambda i: (0, 0)),
            pl.BlockSpec((D_MODEL, wtot), lambda i: (0, 0)),
        ],
        out_specs=[
            pl.BlockSpec((tm, R_PROJ), lambda i: (i, 0)),
            pl.BlockSpec((tm, M_PROJ_PAD), lambda i: (i, 0)),
        ],
        out_shape=[
            jax.ShapeDtypeStruct((n, R_PROJ), F32),
            jax.ShapeDtypeStruct((n, M_PROJ_PAD), F32),
        ],
        compiler_params=pltpu.CompilerParams(
            dimension_semantics=("arbitrary",), vmem_limit_bytes=48 * 1024 * 1024),
        name="in_proj",
    )(x2d, norm_w, w_cat_bf16)


def _split3(x):
    hi = x.astype(BF16)
    r1 = x - hi.astype(F32)
    mid = r1.astype(BF16)
    lo = (r1 - mid.astype(F32)).astype(BF16)
    return hi, mid, lo


def _dot_f32_lhs(x, b_exact):
    return sum(jnp.dot(piece, b_exact, preferred_element_type=F32) for piece in _split3(x))


def _dot_f32_rhs(a_exact, x):
    return sum(jnp.dot(a_exact, piece, preferred_element_type=F32) for piece in _split3(x))


def _dot_x3(a, b):
    a_hi = a.astype(BF16)
    a_lo = (a - a_hi.astype(F32)).astype(BF16)
    b_hi = b.astype(BF16)
    b_lo = (b - b_hi.astype(F32)).astype(BF16)
    d = lambda x, y: jnp.dot(x, y, preferred_element_type=F32)
    return d(a_hi, b_hi) + d(a_hi, b_lo) + d(a_lo, b_hi)


def _bdot(a, b):
    return jnp.dot(a.astype(BF16), b.astype(BF16), preferred_element_type=F32)


def _bdot_nt(a, b):
    return lax.dot_general(a.astype(BF16), b.astype(BF16), (((1,), (1,)), ((), ())), preferred_element_type=F32)


def _bdot_tn(a, b):
    return lax.dot_general(a.astype(BF16), b.astype(BF16), (((0,), (0,)), ((), ())), preferred_element_type=F32)


RWKV_ROWS_PER_STEP = 4


def _rwkv_kernel(p_ref, shift0_ref, wkv0_ref, mu_ref, w0_ref, w2_ref, a0_ref, a2_ref, g2_ref, kk_ref, ka_ref,
                 rk_ref, lnw_ref, lnb_ref, o_ref, wkv_out_ref, s_scr, prev_scr):
    c = pl.program_id(1)
    nc = pl.num_programs(1)
    L = CHUNK
    npairs = R_WIDTH // LANES
    nrows = p_ref.shape[0]

    @pl.when(c == 0)
    def _():
        for b in range(nrows):
            prev_scr[b] = jnp.broadcast_to(shift0_ref[b], prev_scr.shape[1:])
        s_scr[...] = wkv0_ref[...]

    li = lax.broadcasted_iota(jnp.int32, (LANES, LANES), 0)
    lj = lax.broadcasted_iota(jnp.int32, (LANES, LANES), 1)
    bd_mask = (li // R_HEAD) == (lj // R_HEAD)
    seg = bd_mask.astype(BF16)

    def head_sums(x):
        return jnp.concatenate([_dot_f32_lhs(x[:, LANES * j:LANES * (j + 1)], seg) for j in range(npairs)], axis=-1)

    ti = lax.broadcasted_iota(jnp.int32, (L, L), 0)
    si = lax.broadcasted_iota(jnp.int32, (L, L), 1)
    tri = (si <= ti).astype(BF16)
    row = lax.broadcasted_iota(jnp.int32, (L, 1), 0)

    rows = []
    for b in range(nrows):
        p = p_ref[b]
        p_prev = jnp.where(row == 0, prev_scr[b, 0:1, :], pltpu.roll(p, 1, 0))
        prev_scr[b] = jnp.broadcast_to(p[L - 1:L, :], prev_scr.shape[1:])
        pm = p + (p_prev - p) * mu_ref[...]
        r = pm[:, 0:R_WIDTH]
        k = pm[:, R_WIDTH:2 * R_WIDTH]
        v = pm[:, 2 * R_WIDTH:3 * R_WIDTH]
        wa = pm[:, 3 * R_WIDTH:3 * R_WIDTH + LANES]
        gd = pm[:, 3 * R_WIDTH + LANES:3 * R_WIDTH + 2 * LANES]
        w_log = -_softplus(-(w0_ref[...] + _dot_x3(jnp.tanh(wa), w2_ref[...]))) - 0.5
        lw = -jnp.exp(w_log)
        a = _sigmoid(a0_ref[...] + _dot_x3(wa, a2_ref[...]))
        g = _dot_x3(_sigmoid(gd), g2_ref[...])
        kk = k * kk_ref[...]
        kk = kk / jnp.maximum(jnp.sqrt(head_sums(kk * kk)), 1e-12)
        k_mod = k * (1.0 + (a - 1.0) * ka_ref[...])
        cl = _dot_f32_rhs(tri, lw)
        e_in = jnp.exp(cl)
        e_neg = jnp.exp(-cl)
        rows.append(dict(At=-kk * jnp.exp(cl - lw), Bt=kk * a * e_neg, Kt=k_mod * e_neg, Rt=r * e_in, v=v,
                         p_last=e_in[L - 1:L, :], bonus=head_sums(r * k_mod * rk_ref[...]) * v, g=g))

    lane = lax.broadcasted_iota(jnp.int32, (L, LANES), 1)
    trow = lax.broadcasted_iota(jnp.int32, (L, LANES), 0)
    first = lane < R_HEAD
    strict = (lane % R_HEAD) < trow
    incl = (lane % R_HEAD) <= trow
    zeros = jnp.zeros((L, LANES), F32)

    units = [(b, j) for b in range(nrows) for j in range(npairs)]
    tile = lambda b, j, name: rows[b][name][:, LANES * j:LANES * (j + 1)]
    S = [s_scr[b, j] for b, j in units]
    bk = [jnp.concatenate([tile(b, j, "Bt"), tile(b, j, "Kt")], axis=0).astype(BF16) for b, j in units]
    G = []
    for u, (b, j) in enumerate(units):
        A_, R_ = tile(b, j, "At"), tile(b, j, "Rt")
        x4 = jnp.concatenate([jnp.where(first, A_, 0.0), jnp.where(first, 0.0, A_),
                              jnp.where(first, R_, 0.0), jnp.where(first, 0.0, R_)], axis=0)
        G.append(_bdot_nt(x4, bk[u]))
    ss = [_bdot_nt(jnp.concatenate([tile(b, j, "At"), tile(b, j, "Rt")], axis=0), S[u])
          for u, (b, j) in enumerate(units)]
    m0 = [jnp.where(strict, g_[0:L], 0.0) for g_ in G]
    m1 = [jnp.where(strict, g_[L:2 * L], 0.0) for g_ in G]
    x = []
    for u, (b, j) in enumerate(units):
        zv = jnp.concatenate([zeros, tile(b, j, "v")], axis=0).astype(BF16)
        x.append(ss[u][0:L] + jnp.where(first, _bdot(m0[u], zv), _bdot(m1[u], zv)))
    n0 = [m[:, 0:R_HEAD].astype(BF16) for m in m0]
    n1 = [m[:, 0:R_HEAD].astype(BF16) for m in m1]
    for i in range(6):
        for u in range(len(units)):
            xb = x[u].astype(BF16)
            x[u] = x[u] + jnp.where(first, _bdot(n0[u], xb), _bdot(n1[u], xb))
        if i < 5:
            n0 = [_bdot(n, n).astype(BF16) for n in n0]
            n1 = [_bdot(n, n).astype(BF16) for n in n1]
    outs = [[None] * npairs for _ in range(nrows)]
    for u, (b, j) in enumerate(units):
        uv = jnp.concatenate([x[u], tile(b, j, "v")], axis=0).astype(BF16)
        mo0 = jnp.where(incl, G[u][2 * L:3 * L], 0.0)
        mo1 = jnp.where(incl, G[u][3 * L:4 * L], 0.0)
        outs[b][j] = ss[u][L:2 * L] + jnp.where(first, _bdot(mo0, uv), _bdot(mo1, uv))
        S[u] = (S[u] + jnp.where(bd_mask, _bdot_tn(uv, bk[u]), 0.0)) * tile(b, j, "p_last")
    for u, (b, j) in enumerate(units):
        s_scr[b, j] = S[u]
    for b in range(nrows):
        o = jnp.concatenate(outs[b], axis=-1)
        mean = head_sums(o) * (1.0 / R_HEAD)
        d = o - mean
        var = head_sums(d * d) * (1.0 / R_HEAD)
        o = d * lax.rsqrt(var + R_GN_EPS)
        o_ref[b] = (o * lnw_ref[...] + lnb_ref[...] + rows[b]["bonus"]) * rows[b]["g"]

    @pl.when(c == nc - 1)
    def _():
        wkv_out_ref[...] = s_scr[...]


def _rwkv(proj_r, shift0, wkv0, wts):
    bsz, t, _ = proj_r.shape
    nc = t // CHUNK
    npairs = R_WIDTH // LANES
    w5 = wkv0.reshape(bsz, npairs, 2, R_HEAD, R_HEAD)
    zero = jnp.zeros_like(w5[:, :, 0])
    wkv_bd = jnp.concatenate([jnp.concatenate([w5[:, :, 0], zero], axis=-1),
                              jnp.concatenate([zero, w5[:, :, 1]], axis=-1)], axis=-2)
    full = lambda shape: pl.BlockSpec(shape, lambda b, c: (0,) * len(shape))
    nb = math.gcd(RWKV_ROWS_PER_STEP, bsz)
    o_r, s_bd = pl.pallas_call(
        _rwkv_kernel,
        grid=(bsz // nb, nc),
        in_specs=[
            pl.BlockSpec((nb, CHUNK, R_PROJ), lambda b, c: (b, c, 0)),
            pl.BlockSpec((nb, 1, R_PROJ), lambda b, c: (b, 0, 0)),
            pl.BlockSpec((nb, npairs, LANES, LANES), lambda b, c: (b, 0, 0, 0)),
            full((1, R_PROJ)), full((1, R_WIDTH)), full((LANES, R_WIDTH)), full((1, R_WIDTH)),
            full((LANES, R_WIDTH)), full((GATE_LORA, R_WIDTH)), full((1, R_WIDTH)), full((1, R_WIDTH)),
            full((1, R_WIDTH)), full((1, R_WIDTH)), full((1, R_WIDTH)),
        ],
        out_specs=[
            pl.BlockSpec((nb, CHUNK, R_WIDTH), lambda b, c: (b, c, 0)),
            pl.BlockSpec((nb, npairs, LANES, LANES), lambda b, c: (b, 0, 0, 0)),
        ],
        out_shape=[
            jax.ShapeDtypeStruct((bsz, t, R_WIDTH), F32),
            jax.ShapeDtypeStruct((bsz, npairs, LANES, LANES), F32),
        ],
        scratch_shapes=[pltpu.VMEM((nb, npairs, LANES, LANES), F32), pltpu.VMEM((nb, SUBLANES, R_PROJ), F32)],
        compiler_params=pltpu.CompilerParams(
            dimension_semantics=("arbitrary", "arbitrary"), vmem_limit_bytes=40 * 1024 * 1024),
        name="rwkv",
    )(proj_r, shift0.reshape(bsz, 1, R_PROJ), wkv_bd, *wts)
    new_wkv = jnp.stack([s_bd[:, :, :R_HEAD, :R_HEAD], s_bd[:, :, R_HEAD:, R_HEAD:]], axis=2)
    return o_r, new_wkv.reshape(bsz, R_HEADS, R_HEAD, R_HEAD)


SSD_ROWS_PER_STEP = 1


def _ssd_kernel(p_ref, conv0_ref, ssm0_ref, cw_ref, cb_ref, dtb_ref, alog_ref, dvec_ref, nw_ref,
                o_ref, conv_out_ref, ssm_out_ref, h_scr, prev_scr):
    c = pl.program_id(1)
    nc = pl.num_programs(1)
    L = CHUNK

    nrows = p_ref.shape[0]
    hpg = M_HEADS // M_GROUPS

    @pl.when(c == 0)
    def _():
        prev_scr[...] = conv0_ref[...]
        h_scr[...] = ssm0_ref[...]

    row8 = lax.broadcasted_iota(jnp.int32, (SUBLANES, 1), 0)
    ti = lax.broadcasted_iota(jnp.int32, (L, L), 0)
    si = lax.broadcasted_iota(jnp.int32, (L, L), 1)
    causal = si <= ti
    tri = causal.astype(BF16)
    A = -jnp.exp(alog_ref[...])
    rows = []
    for b in range(nrows):
        p = p_ref[b]
        xbc = p[:, M_WIDTH:M_WIDTH + CONV_DIM]
        prev8 = prev_scr[b]
        conv = xbc * cw_ref[CONV_W - 1:CONV_W, :]
        for j in range(1, CONV_W):
            xs_j = pltpu.roll(xbc, j, 0)
            first = jnp.where(row8 >= j, xs_j[0:SUBLANES], pltpu.roll(prev8, j, 0))
            shifted = jnp.concatenate([first, xs_j[SUBLANES:]], axis=0)
            conv = conv + shifted * cw_ref[CONV_W - 1 - j:CONV_W - j, :]
        prev_scr[b] = xbc[L - SUBLANES:L, :]
        act = _silu(conv + cb_ref[...])
        dt = _softplus(p[:, M_WIDTH + CONV_DIM:] + dtb_ref[...])
        cs = _dot_f32_rhs(tri, dt * A)
        rows.append(dict(z=p[:, 0:M_WIDTH], xs=act[:, 0:M_WIDTH], Bm=act[:, M_WIDTH:M_WIDTH + M_GROUPS * M_STATE],
                         Cm=act[:, M_WIDTH + M_GROUPS * M_STATE:], dt=dt, cs=cs, cs_t=cs.T))

    units = [(b, h) for b in range(nrows) for h in range(M_HEADS)]
    gsl = lambda h: slice((h // hpg) * M_STATE, (h // hpg + 1) * M_STATE)
    Bg = [rows[b]["Bm"][:, gsl(h)].astype(BF16) for b, h in units]
    Cg = [rows[b]["Cm"][:, gsl(h)].astype(BF16) for b, h in units]
    cb = {(b, g): _bdot_nt(rows[b]["Cm"][:, g * M_STATE:(g + 1) * M_STATE], rows[b]["Bm"][:, g * M_STATE:(g + 1) * M_STATE])
          for b in range(nrows) for g in range(M_GROUPS)}
    hin = [h_scr[b, h] for b, h in units]
    xh = [rows[b]["xs"][:, h * M_HEADDIM:(h + 1) * M_HEADDIM] for b, h in units]
    xdt = [xh[u] * rows[b]["dt"][:, h:h + 1] for u, (b, h) in enumerate(units)]
    cs_col = [rows[b]["cs"][:, h:h + 1] for b, h in units]
    cs_last = [rows[b]["cs"][L - 1:L, h:h + 1] for b, h in units]
    y_off = [_bdot_nt(Cg[u], hin[u]) for u in range(len(units))]
    y_diag = []
    for u, (b, h) in enumerate(units):
        seg = jnp.where(causal, cs_col[u] - rows[b]["cs_t"][h:h + 1, :], 0.0)
        decay = jnp.where(causal, jnp.exp(seg), 0.0)
        y_diag.append(_bdot(cb[(b, h // hpg)] * decay, xdt[u]))
    st = [_bdot_tn(xdt[u] * jnp.exp(cs_last[u] - cs_col[u]), Bg[u]) for u in range(len(units))]
    for u, (b, h) in enumerate(units):
        h_scr[b, h] = hin[u] * jnp.exp(cs_last[u]) + st[u]
    gw = M_WIDTH // M_GROUPS
    for b in range(nrows):
        ys = [y_diag[u] + jnp.exp(cs_col[u]) * y_off[u] + dvec_ref[:, h:h + 1] * xh[u]
              for u, (bb, h) in enumerate(units) if bb == b]
        y = jnp.concatenate(ys, axis=-1) * _silu(rows[b]["z"])
        parts = []
        for g in range(M_GROUPS):
            yg = y[:, g * gw:(g + 1) * gw]
            parts.append(yg * lax.rsqrt(jnp.mean(yg * yg, axis=-1, keepdims=True) + NORM_EPS))
        o_ref[b] = jnp.concatenate(parts, axis=-1) * nw_ref[...]

    @pl.when(c == nc - 1)
    def _():
        conv_out_ref[...] = prev_scr[...]
        ssm_out_ref[...] = h_scr[...]


def _ssd(proj_m, conv0_8, ssm0, wts):
    bsz, t, _ = proj_m.shape
    nc = t // CHUNK
    full = lambda shape: pl.BlockSpec(shape, lambda b, c: (0,) * len(shape))
    nb = math.gcd(SSD_ROWS_PER_STEP, bsz)
    return pl.pallas_call(
        _ssd_kernel,
        grid=(bsz // nb, nc),
        in_specs=[
            pl.BlockSpec((nb, CHUNK, M_PROJ_PAD), lambda b, c: (b, c, 0)),
            pl.BlockSpec((nb, SUBLANES, CONV_DIM), lambda b, c: (b, 0, 0)),
            pl.BlockSpec((nb, M_HEADS, M_HEADDIM, M_STATE), lambda b, c: (b, 0, 0, 0)),
            full((CONV_W, CONV_DIM)), full((1, CONV_DIM)), full((1, LANES)), full((1, LANES)),
            full((1, LANES)), full((1, M_WIDTH)),
        ],
        out_specs=[
            pl.BlockSpec((nb, CHUNK, M_WIDTH), lambda b, c: (b, c, 0)),
            pl.BlockSpec((nb, SUBLANES, CONV_DIM), lambda b, c: (b, 0, 0)),
            pl.BlockSpec((nb, M_HEADS, M_HEADDIM, M_STATE), lambda b, c: (b, 0, 0, 0)),
        ],
        out_shape=[
            jax.ShapeDtypeStruct((bsz, t, M_WIDTH), F32),
            jax.ShapeDtypeStruct((bsz, SUBLANES, CONV_DIM), F32),
            jax.ShapeDtypeStruct((bsz, M_HEADS, M_HEADDIM, M_STATE), F32),
        ],
        scratch_shapes=[pltpu.VMEM((nb, M_HEADS, M_HEADDIM, M_STATE), F32), pltpu.VMEM((nb, SUBLANES, CONV_DIM), F32)],
        compiler_params=pltpu.CompilerParams(
            dimension_semantics=("arbitrary", "arbitrary"), vmem_limit_bytes=40 * 1024 * 1024),
        name="ssd",
    )(proj_m, conv0_8, ssm0, *wts)


def _top16_rows(s, rowid, big):
    vals, ids = [], []
    for _ in range(P_TOPK):
        m = jnp.max(s, axis=0, keepdims=True)
        idx = jnp.min(jnp.where(s == m, rowid, big), axis=0, keepdims=True)
        s = jnp.where(rowid == idx, -jnp.inf, s)
        vals.append(m)
        ids.append(idx)
    return vals, ids


def _route_kernel(or_ref, om_ref, x_ref, wr_ref, wm_ref, n2_ref, wq_ref, keys_ref,
                  h_ref, xn_ref, off_ref, gate_ref, sv_scr, si_scr):
    tm = x_ref.shape[0]
    h = (x_ref[...]
         + jnp.dot(or_ref[...].astype(BF16), wr_ref[...], preferred_element_type=F32)
         + jnp.dot(om_ref[...].astype(BF16), wm_ref[...], preferred_element_type=F32))
    h_ref[...] = h.reshape(tm, SUBLANES, LANES)
    xn = h * lax.rsqrt(jnp.mean(h * h, axis=-1, keepdims=True) + NORM_EPS) * n2_ref[...]
    xn_ref[...] = xn.reshape(tm, SUBLANES, LANES)
    xn_b = xn.astype(BF16)
    rowid = lax.broadcasted_iota(jnp.int32, (N_KEYS, tm), 0)

    def sub_body(hd, carry):
        res = []
        for half in range(2):
            q = jnp.dot(xn_b, wq_ref[2 * hd + half], preferred_element_type=F32)
            s_t = _bdot_nt(keys_ref[half], q)
            res.append(_top16_rows(s_t, rowid, N_KEYS))
        for half, (vals, ids) in enumerate(res):
            sv_scr[2 * hd + half] = jnp.concatenate(vals, axis=0)
            si_scr[2 * hd + half] = jnp.concatenate(ids, axis=0)
        return carry

    lax.fori_loop(0, P_HEADS, sub_body, 0)

    r16 = lax.broadcasted_iota(jnp.int32, (16, 1), 0)
    r8 = lax.broadcasted_iota(jnp.int32, (8, 1), 0)

    def head_body(hd, carry):
        sv0, sv1 = sv_scr[2 * hd], sv_scr[2 * hd + 1]
        si0, si1 = si_scr[2 * hd], si_scr[2 * hd + 1]
        cand, cid, flat = [], [], []
        for i in range(8):
            n_i = P_TOPK // (i + 1)
            rows = 16 if i == 0 else 8
            rr = r16 if i == 0 else r8
            cand.append(jnp.where(rr < n_i, sv0[i:i + 1] + sv1[0:rows], -jnp.inf))
            cid.append(si0[i:i + 1] * N_KEYS + si1[0:rows])
            flat.append(jnp.broadcast_to(i * P_TOPK + rr, (rows, tm)))
        cand.append(sv0[8:16] + sv1[0:1])
        cid.append(si0[8:16] * N_KEYS + si1[0:1])
        flat.append(jnp.broadcast_to((8 + r8) * P_TOPK, (8, tm)))
        cand = jnp.concatenate(cand, axis=0)
        cid = jnp.concatenate(cid, axis=0)
        flat = jnp.concatenate(flat, axis=0)
        tops, eids = [], []
        for _ in range(P_TOPK):
            m = jnp.max(cand, axis=0, keepdims=True)
            sel = jnp.min(jnp.where(cand == m, flat, 4 * P_TOPK * P_TOPK), axis=0, keepdims=True)
            hit = flat == sel
            eids.append(jnp.sum(jnp.where(hit, cid, 0), axis=0, keepdims=True))
            cand = jnp.where(hit, -jnp.inf, cand)
            tops.append(m)
        top_s = jnp.concatenate(tops, axis=0)
        eid = jnp.concatenate(eids, axis=0)
        ex = jnp.exp(top_s - top_s[0:1])
        gate = ex / jnp.sum(ex, axis=0, keepdims=True)
        si_scr[hd] = TAB_PAD + TAB_ROWS_PER_EXPERT * eid
        sv_scr[hd] = gate
        return carry

    lax.fori_loop(0, P_HEADS, head_body, 0)
    off_ref[...] = jnp.concatenate([si_scr[hd] for hd in range(P_HEADS)], axis=0).T
    gate_ref[...] = jnp.concatenate([sv_scr[hd] for hd in range(P_HEADS)], axis=0).T


def _route(o_r, o_m, x2d, w_out_r, w_out_m, norm2_w, wq3, keys, tm=512):
    n = x2d.shape[0]
    tm = math.gcd(tm, n)
    full = lambda shape: pl.BlockSpec(shape, lambda i: (0,) * len(shape))
    return pl.pallas_call(
        _route_kernel,
        grid=(n // tm,),
        in_specs=[
            pl.BlockSpec((tm, R_WIDTH), lambda i: (i, 0)),
            pl.BlockSpec((tm, M_WIDTH), lambda i: (i, 0)),
            pl.BlockSpec((tm, D_MODEL), lambda i: (i, 0)),
            full((R_WIDTH, D_MODEL)), full((M_WIDTH, D_MODEL)), full((1, D_MODEL)),
            full((2 * P_HEADS, D_MODEL, P_QDIM // 2)), full((2, N_KEYS, P_QDIM // 2)),
        ],
        out_specs=[
            pl.BlockSpec((tm, SUBLANES, LANES), lambda i: (i, 0, 0)),
            pl.BlockSpec((tm, SUBLANES, LANES), lambda i: (i, 0, 0)),
            pl.BlockSpec((tm, PAIRS), lambda i: (i, 0)),
            pl.BlockSpec((tm, PAIRS), lambda i: (i, 0)),
        ],
        out_shape=[
            jax.ShapeDtypeStruct((n, SUBLANES, LANES), F32),
            jax.ShapeDtypeStruct((n, SUBLANES, LANES), F32),
            jax.ShapeDtypeStruct((n, PAIRS), jnp.int32),
            jax.ShapeDtypeStruct((n, PAIRS), F32),
        ],
        scratch_shapes=[pltpu.VMEM((2 * P_HEADS, P_TOPK, tm), F32), pltpu.VMEM((2 * P_HEADS, P_TOPK, tm), jnp.int32)],
        compiler_params=pltpu.CompilerParams(
            dimension_semantics=("arbitrary",), vmem_limit_bytes=48 * 1024 * 1024),
        name="route",
    )(o_r, o_m, x2d, w_out_r, w_out_m, norm2_w, wq3, keys)


def _pack_table(tab):
    tb = tab.astype(BF16)
    half = D_MODEL // 2
    lo = lax.bitcast_convert_type(tb[:, :half], jnp.uint16).astype(jnp.uint32)
    hi = lax.bitcast_convert_type(tb[:, half:], jnp.uint16).astype(jnp.uint32)
    w = (lo | (hi << 16)).reshape(N_EXPERTS * TAB_ROWS_PER_EXPERT, LANES)
    return jnp.pad(w, ((TAB_PAD, TAB_PAD), (0, 0)))


V_ROWS = 2 * SUBLANES
PEER_TOKENS_PER_STEP = 16
V_KCHUNK = 256
HID_PAD = PEER_TOKENS_PER_STEP
assert HID_PAD % SUBLANES == 0


IDX_SETS = 2


def _idx_copy(off_vmem, idx_smem, idx_sem, row0, s):
    return pltpu.make_async_copy(off_vmem.at[pl.ds(row0, PEER_TOKENS_PER_STEP)], idx_smem.at[s], idx_sem.at[s])


def _staged_index_loop(off_vmem, idx_smem, idx_sem, tb, step):
    rows_per_trip = IDX_SETS * PEER_TOKENS_PER_STEP
    ntrips = tb // rows_per_trip
    for s in range(IDX_SETS):
        _idx_copy(off_vmem, idx_smem, idx_sem, s * PEER_TOKENS_PER_STEP, s).start()

    def trip(k, carry):
        for s in range(IDX_SETS):
            row0 = pl.multiple_of(k * rows_per_trip + s * PEER_TOKENS_PER_STEP, PEER_TOKENS_PER_STEP)
            _idx_copy(off_vmem, idx_smem, idx_sem, row0, s).wait()
            step(row0, [idx_smem.at[s, j] for j in range(PEER_TOKENS_PER_STEP)])

            @pl.when(k + 1 < ntrips)
            def _():
                _idx_copy(off_vmem, idx_smem, idx_sem, row0 + rows_per_trip, s).start()
        return carry

    lax.fori_loop(0, ntrips, trip, 0)


def _merged_words(tab_vmem, offs, p_lo, p_hi, low):
    wa = tab_vmem[pl.ds(offs[p_lo], SUBLANES), :]
    wb = tab_vmem[pl.ds(offs[p_hi] - TAB_ROWS_PER_EXPERT, SUBLANES), :]
    return jnp.where(low, wa, wb)


def _peer_u_kernel(off_ref, x3_ref, gate_ref, tab_vmem, c_ref, r_scr, hid_scr, idx_smem, idx_sem):
    tb = x3_ref.shape[0]
    nsteps = tb // PEER_TOKENS_PER_STEP
    row = lax.broadcasted_iota(jnp.int32, (SUBLANES, LANES), 0)
    low = row < 4
    m2 = (row % 4) < 2
    m1 = (row % 2) == 0
    ones_b = jnp.ones((SUBLANES, LANES), BF16)

    def fold(a, b, sh, m):
        return jnp.where(m, a + pltpu.roll(a, SUBLANES - sh, 0), b + pltpu.roll(b, sh, 0))

    def flush_lane_sums(row0):
        for j in range(PEER_TOKENS_PER_STEP):
            r = r_scr[j]
            hi = r.astype(BF16)
            lo = (r - hi.astype(F32)).astype(BF16)
            sums = (lax.dot_general(ones_b, hi, (((1,), (1,)), ((), ())), preferred_element_type=F32)
                    + lax.dot_general(ones_b, lo, (((1,), (1,)), ((), ())), preferred_element_type=F32))
            hid_scr[pl.ds(row0 + j + HID_PAD - PEER_TOKENS_PER_STEP, 1), :] = sums[0:1, :]

    r_scr[...] = jnp.zeros(r_scr.shape, F32)

    def step(row0, offs):
        flush_lane_sums(row0)
        toks = [row0 + j for j in range(PEER_TOKENS_PER_STEP)]
        x16 = []
        for t in toks:
            bits = pltpu.bitcast(x3_ref[t].astype(BF16).astype(F32), jnp.uint32)
            w03 = (bits >> 16) | pltpu.roll(bits, 4, 0)
            x16.append(pltpu.bitcast(jnp.where(low, w03, pltpu.roll(w03, 4, 0)), BF16))
        for g in range(PAIRS // 8):
            pidx = [g * 8 + i for i in range(8)]
            for j in range(PEER_TOKENS_PER_STEP):
                def prod(i_lo, i_hi):
                    w = _merged_words(tab_vmem, offs[j], pidx[i_lo], pidx[i_hi], low)
                    pw = pltpu.bitcast(pltpu.bitcast(w, BF16) * x16[j], jnp.uint32)
                    return pltpu.bitcast(pw << 16, F32) + pltpu.bitcast(pw & jnp.uint32(0xFFFF0000), F32)
                n1 = fold(prod(0, 4), prod(2, 6), 2, m2)
                n2 = fold(prod(1, 5), prod(3, 7), 2, m2)
                r_scr[j, pl.ds(g * 8, 8), :] = fold(n1, n2, 1, m1)

    _staged_index_loop(off_ref, idx_smem, idx_sem, tb, step)
    flush_lane_sums(tb)
    hid = hid_scr[pl.ds(HID_PAD, tb), :]
    gelu = 0.5 * hid * (1.0 + lax.erf(hid * (1.0 / math.sqrt(2.0))))
    c_ref[...] = gelu * gate_ref[...]


def _peer_u(off, x3, gate, tab, tb=256):
    n = x3.shape[0]
    tb = math.gcd(tb, n)
    assert tb % (IDX_SETS * PEER_TOKENS_PER_STEP) == 0
    return pl.pallas_call(
        _peer_u_kernel,
        grid=(n // tb,),
        in_specs=[
            pl.BlockSpec((tb, PAIRS), lambda i: (i, 0)),
            pl.BlockSpec((tb, SUBLANES, LANES), lambda i: (i, 0, 0)),
            pl.BlockSpec((tb, PAIRS), lambda i: (i, 0)),
            pl.BlockSpec(memory_space=pltpu.VMEM),
        ],
        out_specs=pl.BlockSpec((tb, PAIRS), lambda i: (i, 0)),
        out_shape=jax.ShapeDtypeStruct((n, PAIRS), F32),
        scratch_shapes=[pltpu.VMEM((PEER_TOKENS_PER_STEP, PAIRS, LANES), F32),
                        pltpu.VMEM((tb + HID_PAD, PAIRS), F32),
                        pltpu.SMEM((IDX_SETS, PEER_TOKENS_PER_STEP, PAIRS), jnp.int32),
                        pltpu.SemaphoreType.DMA((IDX_SETS,))],
        compiler_params=pltpu.CompilerParams(
            dimension_semantics=("arbitrary",), vmem_limit_bytes=56 * 1024 * 1024),
        name="peer_u",
    )(off, x3, gate, tab)


def _peer_v_kernel(off_ref, c_ref, h3_ref, fw_ref, tab_vmem, y_ref, crep_scr, idx_smem, idx_sem):
    tb = h3_ref.shape[0]
    kw = PAIRS * SUBLANES
    fw = fw_ref[...]
    low = lax.broadcasted_iota(jnp.int32, (SUBLANES, LANES), 0) < 4
    rp = lax.broadcasted_iota(jnp.int32, (PAIRS, kw), 0)
    rc = lax.broadcasted_iota(jnp.int32, (PAIRS, kw), 1)
    crep_scr[...] = _dot_f32_lhs(c_ref[...], (rc // SUBLANES == rp).astype(BF16))
    m = lax.broadcasted_iota(jnp.int32, (SUBLANES, kw), 0)
    col = lax.broadcasted_iota(jnp.int32, (SUBLANES, kw), 1)
    sel = m == (col % 2) * 4 + ((col % V_ROWS) // 2) % 4

    def step(row0, offs):
        toks = [row0 + j for j in range(PEER_TOKENS_PER_STEP)]
        x16 = []
        for t in toks:
            lhs = jnp.where(sel, crep_scr[pl.ds(t, 1), :], 0.0)
            hi = pltpu.bitcast(pltpu.bitcast(lhs, jnp.uint32) & jnp.uint32(0xFFFF0000), F32)
            x16.append(jnp.concatenate([hi, lhs - hi], axis=0).astype(BF16))
        acc = [None] * len(toks)
        per_chunk = V_KCHUNK // V_ROWS
        for kc in range(kw // V_KCHUNK):
            for j in range(len(toks)):
                tiles = [pltpu.bitcast(_merged_words(tab_vmem, offs[j], 2 * mm, 2 * mm + 1, low), BF16)
                         for mm in range(kc * per_chunk, (kc + 1) * per_chunk)]
                part = jnp.dot(x16[j][:, kc * V_KCHUNK:(kc + 1) * V_KCHUNK], jnp.concatenate(tiles, axis=0),
                               preferred_element_type=F32)
                acc[j] = part if kc == 0 else acc[j] + part
        for j, t in enumerate(toks):
            y_ref[t] = acc[j][0:SUBLANES] + acc[j][SUBLANES:]

    _staged_index_loop(off_ref, idx_smem, idx_sem, tb, step)
    hn = h3_ref[...] + y_ref[...]
    ms = jnp.sum(jnp.sum(hn * hn, axis=2, keepdims=True), axis=1, keepdims=True) * (1.0 / D_MODEL)
    y_ref[...] = hn * lax.rsqrt(ms + NORM_EPS) * fw


def _peer_v(off, c, h3, fw, tab, tb=256):
    n = h3.shape[0]
    tb = math.gcd(tb, n)
    assert tb % (IDX_SETS * PEER_TOKENS_PER_STEP) == 0
    return pl.pallas_call(
        _peer_v_kernel,
        grid=(n // tb,),
        in_specs=[
            pl.BlockSpec((tb, PAIRS), lambda i: (i, 0)),
            pl.BlockSpec((tb, PAIRS), lambda i: (i, 0)),
            pl.BlockSpec((tb, SUBLANES, LANES), lambda i: (i, 0, 0)),
            pl.BlockSpec((SUBLANES, LANES), lambda i: (0, 0)),
            pl.BlockSpec(memory_space=pltpu.VMEM),
        ],
        out_specs=pl.BlockSpec((tb, SUBLANES, LANES), lambda i: (i, 0, 0)),
        out_shape=jax.ShapeDtypeStruct((n, SUBLANES, LANES), F32),
        scratch_shapes=[pltpu.VMEM((tb, PAIRS * SUBLANES), F32),
                        pltpu.SMEM((IDX_SETS, PEER_TOKENS_PER_STEP, PAIRS), jnp.int32),
                        pltpu.SemaphoreType.DMA((IDX_SETS,))],
        compiler_params=pltpu.CompilerParams(
            dimension_semantics=("arbitrary",), vmem_limit_bytes=56 * 1024 * 1024),
        name="peer_v",
    )(off, c, h3, fw, tab)


def _prep_weights(norm1_w, w_in, rwkv_mu, rwkv_w0, rwkv_w2, rwkv_a0, rwkv_a2, rwkv_g2, rwkv_k_k, rwkv_k_a,
                  rwkv_r_k, rwkv_ln_w, rwkv_ln_b, mamba_conv_w, mamba_conv_b, mamba_dt_bias, mamba_A_log,
                  mamba_D, mamba_norm_w, w_out, norm2_w, peer_w_q, peer_sub_keys, peer_u, peer_v, final_norm_w):
    row = lambda a: a.reshape(1, -1)
    pad_lanes = lambda a: jnp.pad(row(a), ((0, 0), (0, LANES - a.shape[-1])))
    w_cat = jnp.pad(w_in, ((0, 0), (0, M_PROJ_PAD - M_PROJ))).astype(BF16)
    zeros = jnp.zeros((DECAY_LORA, R_WIDTH), F32)
    rw = (row(rwkv_mu), row(rwkv_w0), jnp.concatenate([rwkv_w2, zeros], 0), row(rwkv_a0),
          jnp.concatenate([zeros, rwkv_a2], 0), rwkv_g2, row(rwkv_k_k), row(rwkv_k_a), row(rwkv_r_k),
          row(rwkv_ln_w), row(rwkv_ln_b))
    mw = (mamba_conv_w, row(mamba_conv_b), pad_lanes(mamba_dt_bias), pad_lanes(mamba_A_log),
          pad_lanes(mamba_D), row(mamba_norm_w))
    wq3 = peer_w_q.reshape(D_MODEL, 2 * P_HEADS, P_QDIM // 2).transpose(1, 0, 2).astype(BF16)
    route_w = (w_out[:R_WIDTH].astype(BF16), w_out[R_WIDTH:].astype(BF16), row(norm2_w), wq3, peer_sub_keys)
    return dict(norm1=row(norm1_w), w_cat=w_cat, rw=rw, mw=mw, route_w=route_w,
                tab_u=_pack_table(peer_u), tab_v=_pack_table(peer_v),
                fw=final_norm_w.reshape(SUBLANES, LANES))


def _trunk(x, shift0, wkv0, conv0, ssm0, w):
    bsz, t, _ = x.shape
    n = bsz * t
    x2d = x.reshape(n, D_MODEL)
    proj_r, proj_m = _in_proj(x2d, w["norm1"], w["w_cat"])
    proj_r = proj_r.reshape(bsz, t, R_PROJ)
    proj_m = proj_m.reshape(bsz, t, M_PROJ_PAD)
    o_r, new_wkv = _rwkv(proj_r, shift0, wkv0, w["rw"])
    conv0_8 = jnp.pad(conv0, ((0, 0), (SUBLANES - (CONV_W - 1), 0), (0, 0)))
    o_m, new_conv8, new_ssm = _ssd(proj_m, conv0_8, ssm0, w["mw"])
    h, xn, off, gate = _route(o_r.reshape(n, R_WIDTH), o_m.reshape(n, M_WIDTH), x2d, *w["route_w"])
    c = _peer_u(off, xn, gate, w["tab_u"])
    y = _peer_v(off, c, h, w["fw"], w["tab_v"])
    new_shift = proj_r[:, -1, :]
    new_conv = new_conv8[:, SUBLANES - (CONV_W - 1):, :]
    return y.reshape(bsz, t, D_MODEL), new_shift[None], new_wkv[None], new_conv[None], new_ssm[None]


def kernel(x_prompt, x_sample, state_rwkv_shift, state_rwkv_wkv, state_mamba_conv, state_mamba_ssm, norm1_w, w_in, rwkv_mu, rwkv_w0, rwkv_w2, rwkv_a0, rwkv_a2, rwkv_g2, rwkv_k_k, rwkv_k_a, rwkv_r_k, rwkv_ln_w, rwkv_ln_b, mamba_conv_w, mamba_conv_b, mamba_dt_bias, mamba_A_log, mamba_D, mamba_norm_w, w_out, norm2_w, peer_w_q, peer_sub_keys, peer_u, peer_v, final_norm_w):
    w = _prep_weights(norm1_w[0], w_in[0], rwkv_mu[0], rwkv_w0[0], rwkv_w2[0], rwkv_a0[0], rwkv_a2[0], rwkv_g2[0],
                      rwkv_k_k[0], rwkv_k_a[0], rwkv_r_k[0].reshape(-1), rwkv_ln_w[0], rwkv_ln_b[0],
                      mamba_conv_w[0], mamba_conv_b[0], mamba_dt_bias[0], mamba_A_log[0], mamba_D[0],
                      mamba_norm_w[0], w_out[0], norm2_w[0], peer_w_q[0], peer_sub_keys[0], peer_u[0], peer_v[0],
                      final_norm_w)
    bp = x_prompt.shape[0]
    zp = lambda *s: jnp.zeros((bp,) + s, F32)
    yp, *sp = _trunk(x_prompt, zp(R_PROJ), zp(R_HEADS, R_HEAD, R_HEAD), zp(CONV_W - 1, CONV_DIM),
                     zp(M_HEADS, M_HEADDIM, M_STATE), w)
    ys, *ss = _trunk(x_sample, state_rwkv_shift[0], state_rwkv_wkv[0], state_mamba_conv[0], state_mamba_ssm[0], w)
    return (yp, ys, *sp, *ss)
```

```python
import functools
import math

import jax
import jax.numpy as jnp
import numpy as np
from jax import lax
from jax.experimental import pallas as pl
from jax.experimental.pallas import tpu as pltpu

LANES = 128
SUBLANES = 8
VMEM_BYTES_V7X = 64 * 1024 * 1024

D_MODEL = 1024
CHUNK = 64
R_WIDTH = 512
R_HEAD = 64
R_HEADS = R_WIDTH // R_HEAD
DECAY_LORA = 64
AAA_LORA = 64
GATE_LORA = 128
R_PROJ = 3 * R_WIDTH + DECAY_LORA + AAA_LORA + GATE_LORA
R_GN_EPS = R_HEAD * 1e-5
M_WIDTH = 512
M_HEADDIM = 64
M_HEADS = M_WIDTH // M_HEADDIM
M_GROUPS = 2
M_STATE = 64
CONV_W = 4
CONV_DIM = M_WIDTH + 2 * M_GROUPS * M_STATE
M_PROJ = M_WIDTH + CONV_DIM + M_HEADS
M_PROJ_PAD = M_WIDTH + CONV_DIM + LANES
N_KEYS = 128
N_EXPERTS = N_KEYS * N_KEYS
P_HEADS = 8
P_QDIM = 256
P_TOPK = 16
PAIRS = P_HEADS * P_TOPK
NORM_EPS = 1e-6

HI = lax.Precision.HIGHEST
F32 = jnp.float32
BF16 = jnp.bfloat16

TAB_ROWS_PER_EXPERT = 4
TAB_PAD = 8
TAB_ROWS = N_EXPERTS * TAB_ROWS_PER_EXPERT + 2 * TAB_PAD


def _dot(a, b, precision=HI):
    return jnp.dot(a, b, precision=precision, preferred_element_type=F32)


def _dot_nt(a, b, precision=HI):
    return lax.dot_general(a, b, (((1,), (1,)), ((), ())), precision=precision, preferred_element_type=F32)


def _dot_tn(a, b, precision=HI):
    return lax.dot_general(a, b, (((0,), (0,)), ((), ())), precision=precision, preferred_element_type=F32)


def _softplus(x):
    return jnp.maximum(x, 0.0) + jnp.log(1.0 + jnp.exp(-jnp.abs(x)))


def _sigmoid(x):
    return 1.0 / (1.0 + jnp.exp(-x))


def _silu(x):
    return x * _sigmoid(x)


def _in_proj_kernel(x_ref, nw_ref, w_ref, pr_ref, pm_ref):
    x = x_ref[...]
    xn = x * lax.rsqrt(jnp.mean(x * x, axis=-1, keepdims=True) + NORM_EPS) * nw_ref[...]
    p = jnp.dot(xn.astype(BF16), w_ref[...], preferred_element_type=F32)
    pr_ref[...] = p[:, :R_PROJ]
    pm_ref[...] = p[:, R_PROJ:]


def _in_proj(x2d, norm_w, w_cat_bf16, tm=512):
    n = x2d.shape[0]
    tm = math.gcd(tm, n)
    wtot = R_PROJ + M_PROJ_PAD
    return pl.pallas_call(
        _in_proj_kernel,
        grid=(n // tm,),
        in_specs=[
            pl.BlockSpec((tm, D_MODEL), lambda i: (i, 0)),
            pl.BlockSpec((1, D_MODEL), lambda i: (0, 0)),
            pl.BlockSpec((D_MODEL, wtot), lambda i: (0, 0)),
        ],
        out_specs=[
            pl.BlockSpec((tm, R_PROJ), lambda i: (i, 0)),
            pl.BlockSpec((tm, M_PROJ_PAD), lambda i: (i, 0)),
        ],
        out_shape=[
            jax.ShapeDtypeStruct((n, R_PROJ), F32),
            jax.ShapeDtypeStruct((n, M_PROJ_PAD), F32),
        ],
        compiler_params=pltpu.CompilerParams(
            dimension_semantics=("arbitrary",), vmem_limit_bytes=48 * 1024 * 1024),
        name="in_proj",
    )(x2d, norm_w, w_cat_bf16)


def _split3(x):
    hi = x.astype(BF16)
    r1 = x - hi.astype(F32)
    mid = r1.astype(BF16)
    lo = (r1 - mid.astype(F32)).astype(BF16)
    return hi, mid, lo


def _dot_f32_lhs(x, b_exact):
    return sum(jnp.dot(piece, b_exact, preferred_element_type=F32) for piece in _split3(x))


def _dot_f32_rhs(a_exact, x):
    return sum(jnp.dot(a_exact, piece, preferred_element_type=F32) for piece in _split3(x))


def _dot_x3(a, b):
    a_hi = a.astype(BF16)
    a_lo = (a - a_hi.astype(F32)).astype(BF16)
    b_hi = b.astype(BF16)
    b_lo = (b - b_hi.astype(F32)).astype(BF16)
    d = lambda x, y: jnp.dot(x, y, preferred_element_type=F32)
    return d(a_hi, b_hi) + d(a_hi, b_lo) + d(a_lo, b_hi)


def _bdot(a, b):
    return jnp.dot(a.astype(BF16), b.astype(BF16), preferred_element_type=F32)


def _bdot_nt(a, b):
    return lax.dot_general(a.astype(BF16), b.astype(BF16), (((1,), (1,)), ((), ())), preferred_element_type=F32)


def _bdot_tn(a, b):
    return lax.dot_general(a.astype(BF16), b.astype(BF16), (((0,), (0,)), ((), ())), preferred_element_type=F32)


RWKV_ROWS_PER_STEP = 4


def _rwkv_kernel(p_ref, shift0_ref, wkv0_ref, mu_ref, w0_ref, w2_ref, a0_ref, a2_ref, g2_ref, kk_ref, ka_ref,
                 rk_ref, lnw_ref, lnb_ref, o_ref, wkv_out_ref, s_scr, prev_scr):
    c = pl.program_id(1)
    nc = pl.num_programs(1)
    L = CHUNK
    npairs = R_WIDTH // LANES
    nrows = p_ref.shape[0]

    @pl.when(c == 0)
    def _():
        for b in range(nrows):
            prev_scr[b] = jnp.broadcast_to(shift0_ref[b], prev_scr.shape[1:])
        s_scr[...] = wkv0_ref[...]

    li = lax.broadcasted_iota(jnp.int32, (LANES, LANES), 0)
    lj = lax.broadcasted_iota(jnp.int32, (LANES, LANES), 1)
    bd_mask = (li // R_HEAD) == (lj // R_HEAD)
    seg = bd_mask.astype(BF16)

    def head_sums(x):
        return jnp.concatenate([_dot_f32_lhs(x[:, LANES * j:LANES * (j + 1)], seg) for j in range(npairs)], axis=-1)

    ti = lax.broadcasted_iota(jnp.int32, (L, L), 0)
    si = lax.broadcasted_iota(jnp.int32, (L, L), 1)
    tri = (si <= ti).astype(BF16)
    row = lax.broadcasted_iota(jnp.int32, (L, 1), 0)

    rows = []
    for b in range(nrows):
        p = p_ref[b]
        p_prev = jnp.where(row == 0, prev_scr[b, 0:1, :], pltpu.roll(p, 1, 0))
        prev_scr[b] = jnp.broadcast_to(p[L - 1:L, :], prev_scr.shape[1:])
        pm = p + (p_prev - p) * mu_ref[...]
        r = pm[:, 0:R_WIDTH]
        k = pm[:, R_WIDTH:2 * R_WIDTH]
        v = pm[:, 2 * R_WIDTH:3 * R_WIDTH]
        wa = pm[:, 3 * R_WIDTH:3 * R_WIDTH + LANES]
        gd = pm[:, 3 * R_WIDTH + LANES:3 * R_WIDTH + 2 * LANES]
        w_log = -_softplus(-(w0_ref[...] + _dot_x3(jnp.tanh(wa), w2_ref[...]))) - 0.5
        lw = -jnp.exp(w_log)
        a = _sigmoid(a0_ref[...] + _dot_x3(wa, a2_ref[...]))
        g = _dot_x3(_sigmoid(gd), g2_ref[...])
        kk = k * kk_ref[...]
        kk = kk / jnp.maximum(jnp.sqrt(head_sums(kk * kk)), 1e-12)
        k_mod = k * (1.0 + (a - 1.0) * ka_ref[...])
        cl = _dot_f32_rhs(tri, lw)
        e_in = jnp.exp(cl)
        e_neg = jnp.exp(-cl)
        rows.append(dict(At=-kk * jnp.exp(cl - lw), Bt=kk * a * e_neg, Kt=k_mod * e_neg, Rt=r * e_in, v=v,
                         p_last=e_in[L - 1:L, :], bonus=head_sums(r * k_mod * rk_ref[...]) * v, g=g))

    lane = lax.broadcasted_iota(jnp.int32, (L, LANES), 1)
    trow = lax.broadcasted_iota(jnp.int32, (L, LANES), 0)
    first = lane < R_HEAD
    strict = (lane % R_HEAD) < trow
    incl = (lane % R_HEAD) <= trow
    zeros = jnp.zeros((L, LANES), F32)

    units = [(b, j) for b in range(nrows) for j in range(npairs)]
    tile = lambda b, j, name: rows[b][name][:, LANES * j:LANES * (j + 1)]
    S = [s_scr[b, j] for b, j in units]
    bk = [jnp.concatenate([tile(b, j, "Bt"), tile(b, j, "Kt")], axis=0).astype(BF16) for b, j in units]
    G = []
    for u, (b, j) in enumerate(units):
        A_, R_ = tile(b, j, "At"), tile(b, j, "Rt")
        x4 = jnp.concatenate([jnp.where(first, A_, 0.0), jnp.where(first, 0.0, A_),
                              jnp.where(first, R_, 0.0), jnp.where(first, 0.0, R_)], axis=0)
        G.append(_bdot_nt(x4, bk[u]))
    ss = [_bdot_nt(jnp.concatenate([tile(b, j, "At"), tile(b, j, "Rt")], axis=0), S[u])
          for u, (b, j) in enumerate(units)]
    m0 = [jnp.where(strict, g_[0:L], 0.0) for g_ in G]
    m1 = [jnp.where(strict, g_[L:2 * L], 0.0) for g_ in G]
    x = []
    for u, (b, j) in enumerate(units):
        zv = jnp.concatenate([zeros, tile(b, j, "v")], axis=0).astype(BF16)
        x.append(ss[u][0:L] + jnp.where(first, _bdot(m0[u], zv), _bdot(m1[u], zv)))
    n0 = [m[:, 0:R_HEAD].astype(BF16) for m in m0]
    n1 = [m[:, 0:R_HEAD].astype(BF16) for m in m1]
    for i in range(6):
        for u in range(len(units)):
            xb = x[u].astype(BF16)
            x[u] = x[u] + jnp.where(first, _bdot(n0[u], xb), _bdot(n1[u], xb))
        if i < 5:
            n0 = [_bdot(n, n).astype(BF16) for n in n0]
            n1 = [_bdot(n, n).astype(BF16) for n in n1]
    outs = [[None] * npairs for _ in range(nrows)]
    for u, (b, j) in enumerate(units):
        uv = jnp.concatenate([x[u], tile(b, j, "v")], axis=0).astype(BF16)
        mo0 = jnp.where(incl, G[u][2 * L:3 * L], 0.0)
        mo1 = jnp.where(incl, G[u][3 * L:4 * L], 0.0)
        outs[b][j] = ss[u][L:2 * L] + jnp.where(first, _bdot(mo0, uv), _bdot(mo1, uv))
        S[u] = (S[u] + jnp.where(bd_mask, _bdot_tn(uv, bk[u]), 0.0)) * tile(b, j, "p_last")
    for u, (b, j) in enumerate(units):
        s_scr[b, j] = S[u]
    for b in range(nrows):
        o = jnp.concatenate(outs[b], axis=-1)
        mean = head_sums(o) * (1.0 / R_HEAD)
        d = o - mean
        var = head_sums(d * d) * (1.0 / R_HEAD)
        o = d * lax.rsqrt(var + R_GN_EPS)
        o_ref[b] = (o * lnw_ref[...] + lnb_ref[...] + rows[b]["bonus"]) * rows[b]["g"]

    @pl.when(c == nc - 1)
    def _():
        wkv_out_ref[...] = s_scr[...]


def _rwkv(proj_r, shift0, wkv0, wts):
    bsz, t, _ = proj_r.shape
    nc = t // CHUNK
    npairs = R_WIDTH // LANES
    w5 = wkv0.reshape(bsz, npairs, 2, R_HEAD, R_HEAD)
    zero = jnp.zeros_like(w5[:, :, 0])
    wkv_bd = jnp.concatenate([jnp.concatenate([w5[:, :, 0], zero], axis=-1),
                              jnp.concatenate([zero, w5[:, :, 1]], axis=-1)], axis=-2)
    full = lambda shape: pl.BlockSpec(shape, lambda b, c: (0,) * len(shape))
    nb = math.gcd(RWKV_ROWS_PER_STEP, bsz)
    o_r, s_bd = pl.pallas_call(
        _rwkv_kernel,
        grid=(bsz // nb, nc),
        in_specs=[
            pl.BlockSpec((nb, CHUNK, R_PROJ), lambda b, c: (b, c, 0)),
            pl.BlockSpec((nb, 1, R_PROJ), lambda b, c: (b, 0, 0)),
            pl.BlockSpec((nb, npairs, LANES, LANES), lambda b, c: (b, 0, 0, 0)),
            full((1, R_PROJ)), full((1, R_WIDTH)), full((LANES, R_WIDTH)), full((1, R_WIDTH)),
            full((LANES, R_WIDTH)), full((GATE_LORA, R_WIDTH)), full((1, R_WIDTH)), full((1, R_WIDTH)),
            full((1, R_WIDTH)), full((1, R_WIDTH)), full((1, R_WIDTH)),
        ],
        out_specs=[
            pl.BlockSpec((nb, CHUNK, R_WIDTH), lambda b, c: (b, c, 0)),
            pl.BlockSpec((nb, npairs, LANES, LANES), lambda b, c: (b, 0, 0, 0)),
        ],
        out_shape=[
            jax.ShapeDtypeStruct((bsz, t, R_WIDTH), F32),
            jax.ShapeDtypeStruct((bsz, npairs, LANES, LANES), F32),
        ],
        scratch_shapes=[pltpu.VMEM((nb, npairs, LANES, LANES), F32), pltpu.VMEM((nb, SUBLANES, R_PROJ), F32)],
        compiler_params=pltpu.CompilerParams(
            dimension_semantics=("arbitrary", "arbitrary"), vmem_limit_bytes=40 * 1024 * 1024),
        name="rwkv",
    )(proj_r, shift0.reshape(bsz, 1, R_PROJ), wkv_bd, *wts)
    new_wkv = jnp.stack([s_bd[:, :, :R_HEAD, :R_HEAD], s_bd[:, :, R_HEAD:, R_HEAD:]], axis=2)
    return o_r, new_wkv.reshape(bsz, R_HEADS, R_HEAD, R_HEAD)


SSD_ROWS_PER_STEP = 1


def _ssd_kernel(p_ref, conv0_ref, ssm0_ref, cw_ref, cb_ref, dtb_ref, alog_ref, dvec_ref, nw_ref,
                o_ref, conv_out_ref, ssm_out_ref, h_scr, prev_scr):
    c = pl.program_id(1)
    nc = pl.num_programs(1)
    L = CHUNK

    nrows = p_ref.shape[0]
    hpg = M_HEADS // M_GROUPS

    @pl.when(c == 0)
    def _():
        prev_scr[...] = conv0_ref[...]
        h_scr[...] = ssm0_ref[...]

    row8 = lax.broadcasted_iota(jnp.int32, (SUBLANES, 1), 0)
    ti = lax.broadcasted_iota(jnp.int32, (L, L), 0)
    si = lax.broadcasted_iota(jnp.int32, (L, L), 1)
    causal = si <= ti
    tri = causal.astype(BF16)
    A = -jnp.exp(alog_ref[...])
    rows = []
    for b in range(nrows):
        p = p_ref[b]
        xbc = p[:, M_WIDTH:M_WIDTH + CONV_DIM]
        prev8 = prev_scr[b]
        conv = xbc * cw_ref[CONV_W - 1:CONV_W, :]
        for j in range(1, CONV_W):
            xs_j = pltpu.roll(xbc, j, 0)
            first = jnp.where(row8 >= j, xs_j[0:SUBLANES], pltpu.roll(prev8, j, 0))
            shifted = jnp.concatenate([first, xs_j[SUBLANES:]], axis=0)
            conv = conv + shifted * cw_ref[CONV_W - 1 - j:CONV_W - j, :]
        prev_scr[b] = xbc[L - SUBLANES:L, :]
        act = _silu(conv + cb_ref[...])
        dt = _softplus(p[:, M_WIDTH + CONV_DIM:] + dtb_ref[...])
        cs = _dot_f32_rhs(tri, dt * A)
        rows.append(dict(z=p[:, 0:M_WIDTH], xs=act[:, 0:M_WIDTH], Bm=act[:, M_WIDTH:M_WIDTH + M_GROUPS * M_STATE],
                         Cm=act[:, M_WIDTH + M_GROUPS * M_STATE:], dt=dt, cs=cs, cs_t=cs.T))

    units = [(b, h) for b in range(nrows) for h in range(M_HEADS)]
    gsl = lambda h: slice((h // hpg) * M_STATE, (h // hpg + 1) * M_STATE)
    Bg = [rows[b]["Bm"][:, gsl(h)].astype(BF16) for b, h in units]
    Cg = [rows[b]["Cm"][:, gsl(h)].astype(BF16) for b, h in units]
    cb = {(b, g): _bdot_nt(rows[b]["Cm"][:, g * M_STATE:(g + 1) * M_STATE], rows[b]["Bm"][:, g * M_STATE:(g + 1) * M_STATE])
          for b in range(nrows) for g in range(M_GROUPS)}
    hin = [h_scr[b, h] for b, h in units]
    xh = [rows[b]["xs"][:, h * M_HEADDIM:(h + 1) * M_HEADDIM] for b, h in units]
    xdt = [xh[u] * rows[b]["dt"][:, h:h + 1] for u, (b, h) in enumerate(units)]
    cs_col = [rows[b]["cs"][:, h:h + 1] for b, h in units]
    cs_last = [rows[b]["cs"][L - 1:L, h:h + 1] for b, h in units]
    y_off = [_bdot_nt(Cg[u], hin[u]) for u in range(len(units))]
    y_diag = []
    for u, (b, h) in enumerate(units):
        seg = jnp.where(causal, cs_col[u] - rows[b]["cs_t"][h:h + 1, :], 0.0)
        decay = jnp.where(causal, jnp.exp(seg), 0.0)
        y_diag.append(_bdot(cb[(b, h // hpg)] * decay, xdt[u]))
    st = [_bdot_tn(xdt[u] * jnp.exp(cs_last[u] - cs_col[u]), Bg[u]) for u in range(len(units))]
    for u, (b, h) in enumerate(units):
        h_scr[b, h] = hin[u] * jnp.exp(cs_last[u]) + st[u]
    gw = M_WIDTH // M_GROUPS
    for b in range(nrows):
        ys = [y_diag[u] + jnp.exp(cs_col[u]) * y_off[u] + dvec_ref[:, h:h + 1] * xh[u]
              for u, (bb, h) in enumerate(units) if bb == b]
        y = jnp.concatenate(ys, axis=-1) * _silu(rows[b]["z"])
        parts = []
        for g in range(M_GROUPS):
            yg = y[:, g * gw:(g + 1) * gw]
            parts.append(yg * lax.rsqrt(jnp.mean(yg * yg, axis=-1, keepdims=True) + NORM_EPS))
        o_ref[b] = jnp.concatenate(parts, axis=-1) * nw_ref[...]

    @pl.when(c == nc - 1)
    def _():
        conv_out_ref[...] = prev_scr[...]
        ssm_out_ref[...] = h_scr[...]


def _ssd(proj_m, conv0_8, ssm0, wts):
    bsz, t, _ = proj_m.shape
    nc = t // CHUNK
    full = lambda shape: pl.BlockSpec(shape, lambda b, c: (0,) * len(shape))
    nb = math.gcd(SSD_ROWS_PER_STEP, bsz)
    return pl.pallas_call(
        _ssd_kernel,
        grid=(bsz // nb, nc),
        in_specs=[
            pl.BlockSpec((nb, CHUNK, M_PROJ_PAD), lambda b, c: (b, c, 0)),
            pl.BlockSpec((nb, SUBLANES, CONV_DIM), lambda b, c: (b, 0, 0)),
            pl.BlockSpec((nb, M_HEADS, M_HEADDIM, M_STATE), lambda b, c: (b, 0, 0, 0)),
            full((CONV_W, CONV_DIM)), full((1, CONV_DIM)), full((1, LANES)), full((1, LANES)),
            full((1, LANES)), full((1, M_WIDTH)),
        ],
        out_specs=[
            pl.BlockSpec((nb, CHUNK, M_WIDTH), lambda b, c: (b, c, 0)),
            pl.BlockSpec((nb, SUBLANES, CONV_DIM), lambda b, c: (b, 0, 0)),
            pl.BlockSpec((nb, M_HEADS, M_HEADDIM, M_STATE), lambda b, c: (b, 0, 0, 0)),
        ],
        out_shape=[
            jax.ShapeDtypeStruct((bsz, t, M_WIDTH), F32),
            jax.ShapeDtypeStruct((bsz, SUBLANES, CONV_DIM), F32),
            jax.ShapeDtypeStruct((bsz, M_HEADS, M_HEADDIM, M_STATE), F32),
        ],
        scratch_shapes=[pltpu.VMEM((nb, M_HEADS, M_HEADDIM, M_STATE), F32), pltpu.VMEM((nb, SUBLANES, CONV_DIM), F32)],
        compiler_params=pltpu.CompilerParams(
            dimension_semantics=("arbitrary", "arbitrary"), vmem_limit_bytes=40 * 1024 * 1024),
        name="ssd",
    )(proj_m, conv0_8, ssm0, *wts)


def _top16_rows(s, rowid, big):
    vals, ids = [], []
    for _ in range(P_TOPK):
        m = jnp.max(s, axis=0, keepdims=True)
        idx = jnp.min(jnp.where(s == m, rowid, big), axis=0, keepdims=True)
        s = jnp.where(rowid == idx, -jnp.inf, s)
        vals.append(m)
        ids.append(idx)
    return vals, ids


def _route_kernel(or_ref, om_ref, x_ref, wr_ref, wm_ref, n2_ref, wq_ref, keys_ref,
                  h_ref, xn_ref, off_ref, gate_ref, sv_scr, si_scr):
    tm = x_ref.shape[0]
    h = (x_ref[...]
         + jnp.dot(or_ref[...].astype(BF16), wr_ref[...], preferred_element_type=F32)
         + jnp.dot(om_ref[...].astype(BF16), wm_ref[...], preferred_element_type=F32))
    h_ref[...] = h.reshape(tm, SUBLANES, LANES)
    xn = h * lax.rsqrt(jnp.mean(h * h, axis=-1, keepdims=True) + NORM_EPS) * n2_ref[...]
    xn_ref[...] = xn.reshape(tm, SUBLANES, LANES)
    xn_b = xn.astype(BF16)
    rowid = lax.broadcasted_iota(jnp.int32, (N_KEYS, tm), 0)

    def sub_body(hd, carry):
        res = []
        for half in range(2):
            q = jnp.dot(xn_b, wq_ref[2 * hd + half], preferred_element_type=F32)
            s_t = _bdot_nt(keys_ref[half], q)
            res.append(_top16_rows(s_t, rowid, N_KEYS))
        for half, (vals, ids) in enumerate(res):
            sv_scr[2 * hd + half] = jnp.concatenate(vals, axis=0)
            si_scr[2 * hd + half] = jnp.concatenate(ids, axis=0)
        return carry

    lax.fori_loop(0, P_HEADS, sub_body, 0)

    r16 = lax.broadcasted_iota(jnp.int32, (16, 1), 0)
    r8 = lax.broadcasted_iota(jnp.int32, (8, 1), 0)

    def head_body(hd, carry):
        sv0, sv1 = sv_scr[2 * hd], sv_scr[2 * hd + 1]
        si0, si1 = si_scr[2 * hd], si_scr[2 * hd + 1]
        cand, cid, flat = [], [], []
        for i in range(8):
            n_i = P_TOPK // (i + 1)
            rows = 16 if i == 0 else 8
            rr = r16 if i == 0 else r8
            cand.append(jnp.where(rr < n_i, sv0[i:i + 1] + sv1[0:rows], -jnp.inf))
            cid.append(si0[i:i + 1] * N_KEYS + si1[0:rows])
            flat.append(jnp.broadcast_to(i * P_TOPK + rr, (rows, tm)))
        cand.append(sv0[8:16] + sv1[0:1])
        cid.append(si0[8:16] * N_KEYS + si1[0:1])
        flat.append(jnp.broadcast_to((8 + r8) * P_TOPK, (8, tm)))
        cand = jnp.concatenate(cand, axis=0)
        cid = jnp.concatenate(cid, axis=0)
        flat = jnp.concatenate(flat, axis=0)
        tops, eids = [], []
        for _ in range(P_TOPK):
            m = jnp.max(cand, axis=0, keepdims=True)
            sel = jnp.min(jnp.where(cand == m, flat, 4 * P_TOPK * P_TOPK), axis=0, keepdims=True)
            hit = flat == sel
            eids.append(jnp.sum(jnp.where(hit, cid, 0), axis=0, keepdims=True))
            cand = jnp.where(hit, -jnp.inf, cand)
            tops.append(m)
        top_s = jnp.concatenate(tops, axis=0)
        eid = jnp.concatenate(eids, axis=0)
        ex = jnp.exp(top_s - top_s[0:1])
        gate = ex / jnp.sum(ex, axis=0, keepdims=True)
        si_scr[hd] = TAB_PAD + TAB_ROWS_PER_EXPERT * eid
        sv_scr[hd] = gate
        return carry

    lax.fori_loop(0, P_HEADS, head_body, 0)
    off_ref[...] = jnp.concatenate([si_scr[hd] for hd in range(P_HEADS)], axis=0).T
    gate_ref[...] = jnp.concatenate([sv_scr[hd] for hd in range(P_HEADS)], axis=0).T


def _route(o_r, o_m, x2d, w_out_r, w_out_m, norm2_w, wq3, keys, tm=512):
    n = x2d.shape[0]
    tm = math.gcd(tm, n)
    full = lambda shape: pl.BlockSpec(shape, lambda i: (0,) * len(shape))
    return pl.pallas_call(
        _route_kernel,
        grid=(n // tm,),
        in_specs=[
            pl.BlockSpec((tm, R_WIDTH), lambda i: (i, 0)),
            pl.BlockSpec((tm, M_WIDTH), lambda i: (i, 0)),
            pl.BlockSpec((tm, D_MODEL), lambda i: (i, 0)),
            full((R_WIDTH, D_MODEL)), full((M_WIDTH, D_MODEL)), full((1, D_MODEL)),
            full((2 * P_HEADS, D_MODEL, P_QDIM // 2)), full((2, N_KEYS, P_QDIM // 2)),
        ],
        out_specs=[
            pl.BlockSpec((tm, SUBLANES, LANES), lambda i: (i, 0, 0)),
            pl.BlockSpec((tm, SUBLANES, LANES), lambda i: (i, 0, 0)),
            pl.BlockSpec((tm, PAIRS), lambda i: (i, 0)),
            pl.BlockSpec((tm, PAIRS), lambda i: (i, 0)),
        ],
        out_shape=[
            jax.ShapeDtypeStruct((n, SUBLANES, LANES), F32),
            jax.ShapeDtypeStruct((n, SUBLANES, LANES), F32),
            jax.ShapeDtypeStruct((n, PAIRS), jnp.int32),
            jax.ShapeDtypeStruct((n, PAIRS), F32),
        ],
        scratch_shapes=[pltpu.VMEM((2 * P_HEADS, P_TOPK, tm), F32), pltpu.VMEM((2 * P_HEADS, P_TOPK, tm), jnp.int32)],
        compiler_params=pltpu.CompilerParams(
            dimension_semantics=("arbitrary",), vmem_limit_bytes=48 * 1024 * 1024),
        name="route",
    )(o_r, o_m, x2d, w_out_r, w_out_m, norm2_w, wq3, keys)


def _pack_table(tab):
    tb = tab.astype(BF16)
    half = D_MODEL // 2
    lo = lax.bitcast_convert_type(tb[:, :half], jnp.uint16).astype(jnp.uint32)
    hi = lax.bitcast_convert_type(tb[:, half:], jnp.uint16).astype(jnp.uint32)
    w = (lo | (hi << 16)).reshape(N_EXPERTS * TAB_ROWS_PER_EXPERT, LANES)
    return jnp.pad(w, ((TAB_PAD, TAB_PAD), (0, 0)))


V_ROWS = 2 * SUBLANES
PEER_TOKENS_PER_STEP = 16
V_KCHUNK = 256
HID_PAD = PEER_TOKENS_PER_STEP
assert HID_PAD % SUBLANES == 0


IDX_SETS = 2


def _idx_copy(off_vmem, idx_smem, idx_sem, row0, s):
    return pltpu.make_async_copy(off_vmem.at[pl.ds(row0, PEER_TOKENS_PER_STEP)], idx_smem.at[s], idx_sem.at[s])


def _staged_index_loop(off_vmem, idx_smem, idx_sem, tb, step):
    rows_per_trip = IDX_SETS * PEER_TOKENS_PER_STEP
    ntrips = tb // rows_per_trip
    for s in range(IDX_SETS):
        _idx_copy(off_vmem, idx_smem, idx_sem, s * PEER_TOKENS_PER_STEP, s).start()

    def trip(k, carry):
        for s in range(IDX_SETS):
            row0 = pl.multiple_of(k * rows_per_trip + s * PEER_TOKENS_PER_STEP, PEER_TOKENS_PER_STEP)
            _idx_copy(off_vmem, idx_smem, idx_sem, row0, s).wait()
            step(row0, [idx_smem.at[s, j] for j in range(PEER_TOKENS_PER_STEP)])

            @pl.when(k + 1 < ntrips)
            def _():
                _idx_copy(off_vmem, idx_smem, idx_sem, row0 + rows_per_trip, s).start()
        return carry

    lax.fori_loop(0, ntrips, trip, 0)


def _merged_words(tab_vmem, offs, p_lo, p_hi, low):
    wa = tab_vmem[pl.ds(offs[p_lo], SUBLANES), :]
    wb = tab_vmem[pl.ds(offs[p_hi] - TAB_ROWS_PER_EXPERT, SUBLANES), :]
    return jnp.where(low, wa, wb)


def _peer_u_kernel(off_ref, x3_ref, gate_ref, tab_vmem, c_ref, r_scr, hid_scr, idx_smem, idx_sem):
    tb = x3_ref.shape[0]
    nsteps = tb // PEER_TOKENS_PER_STEP
    row = lax.broadcasted_iota(jnp.int32, (SUBLANES, LANES), 0)
    low = row < 4
    m2 = (row % 4) < 2
    m1 = (row % 2) == 0
    ones_b = jnp.ones((SUBLANES, LANES), BF16)

    def fold(a, b, sh, m):
        return jnp.where(m, a + pltpu.roll(a, SUBLANES - sh, 0), b + pltpu.roll(b, sh, 0))

    def flush_lane_sums(row0):
        for j in range(PEER_TOKENS_PER_STEP):
            r = r_scr[j]
            hi = r.astype(BF16)
            lo = (r - hi.astype(F32)).astype(BF16)
            sums = (lax.dot_general(ones_b, hi, (((1,), (1,)), ((), ())), preferred_element_type=F32)
                    + lax.dot_general(ones_b, lo, (((1,), (1,)), ((), ())), preferred_element_type=F32))
            hid_scr[pl.ds(row0 + j + HID_PAD - PEER_TOKENS_PER_STEP, 1), :] = sums[0:1, :]

    r_scr[...] = jnp.zeros(r_scr.shape, F32)

    def step(row0, offs):
        flush_lane_sums(row0)
        toks = [row0 + j for j in range(PEER_TOKENS_PER_STEP)]
        x16 = []
        for t in toks:
            bits = pltpu.bitcast(x3_ref[t].astype(BF16).astype(F32), jnp.uint32)
            w03 = (bits >> 16) | pltpu.roll(bits, 4, 0)
            x16.append(pltpu.bitcast(jnp.where(low, w03, pltpu.roll(w03, 4, 0)), BF16))
        for g in range(PAIRS // 8):
            pidx = [g * 8 + i for i in range(8)]
            for j in range(PEER_TOKENS_PER_STEP):
                def prod(i_lo, i_hi):
                    w = _merged_words(tab_vmem, offs[j], pidx[i_lo], pidx[i_hi], low)
                    pw = pltpu.bitcast(pltpu.bitcast(w, BF16) * x16[j], jnp.uint32)
                    return pltpu.bitcast(pw << 16, F32) + pltpu.bitcast(pw & jnp.uint32(0xFFFF0000), F32)
                n1 = fold(prod(0, 4), prod(2, 6), 2, m2)
                n2 = fold(prod(1, 5), prod(3, 7), 2, m2)
                r_scr[j, pl.ds(g * 8, 8), :] = fold(n1, n2, 1, m1)

    _staged_index_loop(off_ref, idx_smem, idx_sem, tb, step)
    flush_lane_sums(tb)
    hid = hid_scr[pl.ds(HID_PAD, tb), :]
    gelu = 0.5 * hid * (1.0 + lax.erf(hid * (1.0 / math.sqrt(2.0))))
    c_ref[...] = gelu * gate_ref[...]


def _peer_u(off, x3, gate, tab, tb=256):
    n = x3.shape[0]
    tb = math.gcd(tb, n)
    assert tb % (IDX_SETS * PEER_TOKENS_PER_STEP) == 0
    return pl.pallas_call(
        _peer_u_kernel,
        grid=(n // tb,),
        in_specs=[
            pl.BlockSpec((tb, PAIRS), lambda i: (i, 0)),
            pl.BlockSpec((tb, SUBLANES, LANES), lambda i: (i, 0, 0)),
            pl.BlockSpec((tb, PAIRS), lambda i: (i, 0)),
            pl.BlockSpec(memory_space=pltpu.VMEM),
        ],
        out_specs=pl.BlockSpec((tb, PAIRS), lambda i: (i, 0)),
        out_shape=jax.ShapeDtypeStruct((n, PAIRS), F32),
        scratch_shapes=[pltpu.VMEM((PEER_TOKENS_PER_STEP, PAIRS, LANES), F32),
                        pltpu.VMEM((tb + HID_PAD, PAIRS), F32),
                        pltpu.SMEM((IDX_SETS, PEER_TOKENS_PER_STEP, PAIRS), jnp.int32),
                        pltpu.SemaphoreType.DMA((IDX_SETS,))],
        compiler_params=pltpu.CompilerParams(
            dimension_semantics=("arbitrary",), vmem_limit_bytes=56 * 1024 * 1024),
        name="peer_u",
    )(off, x3, gate, tab)


def _peer_v_kernel(off_ref, c_ref, h3_ref, fw_ref, tab_vmem, y_ref, crep_scr, acc_scr, idx_smem, idx_sem):
    tb = h3_ref.shape[0]
    kw = PAIRS * SUBLANES
    fw = fw_ref[...]
    low = lax.broadcasted_iota(jnp.int32, (SUBLANES, LANES), 0) < 4
    rp = lax.broadcasted_iota(jnp.int32, (PAIRS, kw), 0)
    rc = lax.broadcasted_iota(jnp.int32, (PAIRS, kw), 1)
    crep_scr[...] = _dot_f32_lhs(c_ref[...], (rc // SUBLANES == rp).astype(BF16))
    m = lax.broadcasted_iota(jnp.int32, (SUBLANES, kw), 0)
    col = lax.broadcasted_iota(jnp.int32, (SUBLANES, kw), 1)
    sel = m == (col % 2) * 4 + ((col % V_ROWS) // 2) % 4

    def step(row0, offs):
        toks = [row0 + j for j in range(PEER_TOKENS_PER_STEP)]
        x16 = []
        for t in toks:
            lhs = jnp.where(sel, crep_scr[pl.ds(t, 1), :], 0.0)
            hi = pltpu.bitcast(pltpu.bitcast(lhs, jnp.uint32) & jnp.uint32(0xFFFF0000), F32)
            x16.append(jnp.concatenate([hi, lhs - hi], axis=0).astype(BF16))
        acc = [None] * len(toks)
        per_chunk = V_KCHUNK // V_ROWS
        for kc in range(kw // V_KCHUNK):
            for j in range(len(toks)):
                tiles = [pltpu.bitcast(_merged_words(tab_vmem, offs[j], 2 * mm, 2 * mm + 1, low), BF16)
                         for mm in range(kc * per_chunk, (kc + 1) * per_chunk)]
                part = jnp.dot(x16[j][:, kc * V_KCHUNK:(kc + 1) * V_KCHUNK], jnp.concatenate(tiles, axis=0),
                               preferred_element_type=F32)
                acc[j] = part if kc == 0 else acc[j] + part
        for j, t in enumerate(toks):
            acc_scr[t] = acc[j][0:SUBLANES] + acc[j][SUBLANES:]

    _staged_index_loop(off_ref, idx_smem, idx_sem, tb, step)
    hn = h3_ref[...] + acc_scr[...]
    ms = jnp.sum(jnp.sum(hn * hn, axis=2, keepdims=True), axis=1, keepdims=True) * (1.0 / D_MODEL)
    y_ref[...] = (hn * lax.rsqrt(ms + NORM_EPS) * fw).reshape(tb, D_MODEL)


def _peer_v(off, c, h3, fw, tab, tb=256):
    n = h3.shape[0]
    tb = math.gcd(tb, n)
    assert tb % (IDX_SETS * PEER_TOKENS_PER_STEP) == 0
    return pl.pallas_call(
        _peer_v_kernel,
        grid=(n // tb,),
        in_specs=[
            pl.BlockSpec((tb, PAIRS), lambda i: (i, 0)),
            pl.BlockSpec((tb, PAIRS), lambda i: (i, 0)),
            pl.BlockSpec((tb, SUBLANES, LANES), lambda i: (i, 0, 0)),
            pl.BlockSpec((SUBLANES, LANES), lambda i: (0, 0)),
            pl.BlockSpec(memory_space=pltpu.VMEM),
        ],
        out_specs=pl.BlockSpec((tb, D_MODEL), lambda i: (i, 0)),
        out_shape=jax.ShapeDtypeStruct((n, D_MODEL), F32),
        scratch_shapes=[pltpu.VMEM((tb, PAIRS * SUBLANES), F32),
                        pltpu.VMEM((tb, SUBLANES, LANES), F32),
                        pltpu.SMEM((IDX_SETS, PEER_TOKENS_PER_STEP, PAIRS), jnp.int32),
                        pltpu.SemaphoreType.DMA((IDX_SETS,))],
        compiler_params=pltpu.CompilerParams(
            dimension_semantics=("arbitrary",), vmem_limit_bytes=56 * 1024 * 1024),
        name="peer_v",
    )(off, c, h3, fw, tab)


def _prep_weights(norm1_w, w_in, rwkv_mu, rwkv_w0, rwkv_w2, rwkv_a0, rwkv_a2, rwkv_g2, rwkv_k_k, rwkv_k_a,
                  rwkv_r_k, rwkv_ln_w, rwkv_ln_b, mamba_conv_w, mamba_conv_b, mamba_dt_bias, mamba_A_log,
                  mamba_D, mamba_norm_w, w_out, norm2_w, peer_w_q, peer_sub_keys, peer_u, peer_v, final_norm_w):
    row = lambda a: a.reshape(1, -1)
    pad_lanes = lambda a: jnp.pad(row(a), ((0, 0), (0, LANES - a.shape[-1])))
    w_cat = jnp.pad(w_in, ((0, 0), (0, M_PROJ_PAD - M_PROJ))).astype(BF16)
    zeros = jnp.zeros((DECAY_LORA, R_WIDTH), F32)
    rw = (row(rwkv_mu), row(rwkv_w0), jnp.concatenate([rwkv_w2, zeros], 0), row(rwkv_a0),
          jnp.concatenate([zeros, rwkv_a2], 0), rwkv_g2, row(rwkv_k_k), row(rwkv_k_a), row(rwkv_r_k),
          row(rwkv_ln_w), row(rwkv_ln_b))
    mw = (mamba_conv_w, row(mamba_conv_b), pad_lanes(mamba_dt_bias), pad_lanes(mamba_A_log),
          pad_lanes(mamba_D), row(mamba_norm_w))
    wq3 = peer_w_q.reshape(D_MODEL, 2 * P_HEADS, P_QDIM // 2).transpose(1, 0, 2).astype(BF16)
    route_w = (w_out[:R_WIDTH].astype(BF16), w_out[R_WIDTH:].astype(BF16), row(norm2_w), wq3, peer_sub_keys)
    return dict(norm1=row(norm1_w), w_cat=w_cat, rw=rw, mw=mw, route_w=route_w,
                tab_u=_pack_table(peer_u), tab_v=_pack_table(peer_v),
                fw=final_norm_w.reshape(SUBLANES, LANES))


def _trunk(x, shift0, wkv0, conv0, ssm0, w):
    bsz, t, _ = x.shape
    n = bsz * t
    x2d = x.reshape(n, D_MODEL)
    proj_r, proj_m = _in_proj(x2d, w["norm1"], w["w_cat"])
    proj_r = proj_r.reshape(bsz, t, R_PROJ)
    proj_m = proj_m.reshape(bsz, t, M_PROJ_PAD)
    o_r, new_wkv = _rwkv(proj_r, shift0, wkv0, w["rw"])
    conv0_8 = jnp.pad(conv0, ((0, 0), (SUBLANES - (CONV_W - 1), 0), (0, 0)))
    o_m, new_conv8, new_ssm = _ssd(proj_m, conv0_8, ssm0, w["mw"])
    h, xn, off, gate = _route(o_r.reshape(n, R_WIDTH), o_m.reshape(n, M_WIDTH), x2d, *w["route_w"])
    c = _peer_u(off, xn, gate, w["tab_u"])
    y = _peer_v(off, c, h, w["fw"], w["tab_v"])
    new_shift = proj_r[:, -1, :]
    new_conv = new_conv8[:, SUBLANES - (CONV_W - 1):, :]
    return y.reshape(bsz, t, D_MODEL), new_shift[None], new_wkv[None], new_conv[None], new_ssm[None]


def kernel(x_prompt, x_sample, state_rwkv_shift, state_rwkv_wkv, state_mamba_conv, state_mamba_ssm, norm1_w, w_in, rwkv_mu, rwkv_w0, rwkv_w2, rwkv_a0, rwkv_a2, rwkv_g2, rwkv_k_k, rwkv_k_a, rwkv_r_k, rwkv_ln_w, rwkv_ln_b, mamba_conv_w, mamba_conv_b, mamba_dt_bias, mamba_A_log, mamba_D, mamba_norm_w, w_out, norm2_w, peer_w_q, peer_sub_keys, peer_u, peer_v, final_norm_w):
    w = _prep_weights(norm1_w[0], w_in[0], rwkv_mu[0], rwkv_w0[0], rwkv_w2[0], rwkv_a0[0], rwkv_a2[0], rwkv_g2[0],
                      rwkv_k_k[0], rwkv_k_a[0], rwkv_r_k[0].reshape(-1), rwkv_ln_w[0], rwkv_ln_b[0],
                      mamba_conv_w[0], mamba_conv_b[0], mamba_dt_bias[0], mamba_A_log[0], mamba_D[0],
                      mamba_norm_w[0], w_out[0], norm2_w[0], peer_w_q[0], peer_sub_keys[0], peer_u[0], peer_v[0],
                      final_norm_w)
    bp = x_prompt.shape[0]
    zp = lambda *s: jnp.zeros((bp,) + s, F32)
    yp, *sp = _trunk(x_prompt, zp(R_PROJ), zp(R_HEADS, R_HEAD, R_HEAD), zp(CONV_W - 1, CONV_DIM),
                     zp(M_HEADS, M_HEADDIM, M_STATE), w)
    ys, *ss = _trunk(x_sample, state_rwkv_shift[0], state_rwkv_wkv[0], state_mamba_conv[0], state_mamba_ssm[0], w)
    return (yp, ys, *sp, *ss)
```
